```python
import math
import jax, jax.numpy as jnp
from jax import lax
import numpy as np

D_MODEL = 1024
BATCH = 1
SEQ = 16384
DEPTH = 2

NORM_EPS = 1e-6
CONV_K = 4
ATTN_HEADS = 8
ATTN_HEAD_DIM = 64
ATTN_WIDTH = ATTN_HEADS * ATTN_HEAD_DIM
DILATED_PATTERNS = ((128, 1), (512, 4), (2048, 16))
ATTN_BLOCK = 128
LRU_WIDTH = D_MODEL // 2
LRU_BLOCKS = 8
LRU_C = 8.0
AB_IN = 3 * ATTN_WIDTH + 2 * LRU_WIDTH
DN_HEADS = 8
DN_HEAD_DIM = 128
DN_WIDTH = DN_HEADS * DN_HEAD_DIM
DN_CHUNK = 64
DN_IN = 4 * DN_WIDTH + 2 * DN_HEADS
XA_HEADS = 4
XA_HEAD_DIM = D_MODEL // XA_HEADS
N_MEM = 256
D_FF = ((8 * D_MODEL // 3 + 127) // 128) * 128

kernel_name = "hybrid_dilated_attn_rglru_gdn_macaron"


def rmsnorm(x, g):
    xf = x.astype(jnp.float32)
    y = xf * lax.rsqrt(jnp.mean(xf * xf, axis=-1, keepdims=True) + NORM_EPS)
    return (y * g.astype(jnp.float32)).astype(x.dtype)


def swiglu(h, w_in, w_out):
    gate, up = jnp.split(h @ w_in, 2, axis=-1)
    return (jax.nn.silu(gate) * up) @ w_out


def causal_dwconv(x, w):
    C = x.shape[-1]
    return lax.conv_general_dilated(
        x, w[:, None, :].astype(x.dtype), window_strides=(1,),
        padding=((w.shape[0] - 1, 0),), dimension_numbers=('NWC', 'WIO', 'NWC'),
        feature_group_count=C)


def dilated_window_attention(q, k, v, dilation, n_back):
    B, S, H, hd = q.shape
    L = S // dilation
    Bd = B * dilation
    Lp = -(-L // ATTN_BLOCK) * ATTN_BLOCK
    nb = Lp // ATTN_BLOCK

    def split(t):
        t = t.reshape(B, L, dilation, H, hd).transpose(0, 2, 1, 3, 4).reshape(Bd, L, H, hd)
        return jnp.pad(t, ((0, 0), (0, Lp - L), (0, 0), (0, 0))).reshape(Bd, nb, ATTN_BLOCK, H, hd)

    qb, kb, vb = split(q), split(k), split(v)

    def with_prev(t):
        prev = jnp.concatenate([jnp.zeros_like(t[:, :1]), t[:, :-1]], axis=1)
        return jnp.concatenate([prev, t], axis=2)

    kk, vv = with_prev(kb), with_prev(vb)
    s = jnp.einsum('bnqhd,bnkhd->bnhqk', qb, kk).astype(jnp.float32) * (hd ** -0.5)
    qi = jnp.arange(ATTN_BLOCK)[:, None]
    kj = jnp.arange(2 * ATTN_BLOCK)[None, :]
    dist = qi + ATTN_BLOCK - kj
    band = (dist >= 0) & (dist <= n_back)
    not_first = (jnp.arange(nb) > 0)[:, None, None] | (kj >= ATTN_BLOCK)[None]
    valid = band[None] & not_first
    s = jnp.where(valid[None, :, None], s, -jnp.inf)
    m = jnp.max(s, axis=-1, keepdims=True)
    p = jnp.exp(s - m)
    den = jnp.sum(p, axis=-1)
    o = jnp.einsum('bnhqk,bnkhd->bnqhd', p, vv.astype(jnp.float32))
    den_t = jnp.moveaxis(den, 2, 3)
    o = o / den_t[..., None]
    lse = jnp.moveaxis(m[..., 0], 2, 3) + jnp.log(den_t)
    o = o.reshape(Bd, Lp, H, hd)[:, :L].reshape(B, dilation, L, H, hd).transpose(0, 2, 1, 3, 4)
    lse = lse.reshape(Bd, Lp, H)[:, :L].reshape(B, dilation, L, H).transpose(0, 2, 1, 3)
    return o.reshape(B, S, H, hd), lse.reshape(B, S, H)


def linear_scan(a, b):
    def comb(l, r):
        return (l[0] * r[0], r[0] * l[1] + r[1])
    _, h = lax.associative_scan(comb, (a, b), axis=1)
    return h


def attn_lru_mixer(h, w_in, conv_w, conv_b, w_a, b_a, w_x, b_x, lam, w_out):
    B, S, _ = h.shape
    A = ATTN_WIDTH
    q, k, v, xr, gr = jnp.split(h @ w_in, [A, 2 * A, 3 * A, 3 * A + LRU_WIDTH], axis=-1)
    shp = (B, S, ATTN_HEADS, ATTN_HEAD_DIM)
    q, k, v = q.reshape(shp), k.reshape(shp), v.reshape(shp)
    outs, lses = [], []
    for window, dil in DILATED_PATTERNS:
        o, l = dilated_window_attention(q, k, v, dil, window // dil)
        outs.append(o)
        lses.append(l)
    wts = jax.nn.softmax(jnp.stack(lses, 0), axis=0)
    attn = jnp.einsum('gbsh,gbshd->bshd', wts, jnp.stack(outs, 0))
    attn = attn.reshape(B, S, A).astype(h.dtype)
    xc = (causal_dwconv(xr, conv_w) + conv_b).astype(jnp.float32)
    xb = xc.reshape(B, S, LRU_BLOCKS, LRU_WIDTH // LRU_BLOCKS)
    r = jax.nn.sigmoid(jnp.einsum('bsnj,njk->bsnk', xb, w_a.astype(jnp.float32)).reshape(B, S, LRU_WIDTH)
                       + b_a.astype(jnp.float32))
    i = jax.nn.sigmoid(jnp.einsum('bsnj,njk->bsnk', xb, w_x.astype(jnp.float32)).reshape(B, S, LRU_WIDTH)
                       + b_x.astype(jnp.float32))
    log_a = -LRU_C * r * jax.nn.softplus(-lam.astype(jnp.float32))
    a = jnp.exp(log_a)
    mult = jnp.sqrt(-jnp.expm1(2.0 * log_a))
    hs = linear_scan(a, mult * i * xc)
    y = hs.astype(h.dtype) * jax.nn.gelu(gr)
    return jnp.concatenate([attn, y], axis=-1) @ w_out


def l2norm(t):
    return t * lax.rsqrt(jnp.sum(t * t, axis=-1, keepdims=True) + 1e-6)


def gated_delta_rule_chunked(q, k, v, g, beta):
    B, S, H, dk = q.shape
    dv = v.shape[-1]
    C = DN_CHUNK
    N = S // C

    def ch(t):
        t = jnp.moveaxis(t, 2, 1)
        return t.reshape((B, H, N, C) + t.shape[3:])

    q = ch(q * (dk ** -0.5))
    k, v, g, beta = ch(k), ch(v), ch(g), ch(beta)
    gcum = jnp.cumsum(g, axis=-1)
    tril = jnp.tril(jnp.ones((C, C), bool))
    strict = jnp.tril(jnp.ones((C, C), bool), -1)
    decay = jnp.exp(jnp.where(tril, gcum[..., :, None] - gcum[..., None, :], -jnp.inf))
    kb = k * beta[..., None]
    kkt = jnp.einsum('bhnid,bhnjd->bhnij', kb, k) * decay
    amat = jnp.eye(C, dtype=jnp.float32) + jnp.where(strict, kkt, 0.0)
    rhs = jnp.concatenate([v * beta[..., None], kb * jnp.exp(gcum)[..., None]], axis=-1)
    sol = lax.linalg.triangular_solve(amat, rhs, left_side=True, lower=True, unit_diagonal=True)
    u, w = sol[..., :dv], sol[..., dv:]
    qk = jnp.einsum('bhnid,bhnjd->bhnij', q, k) * decay

    def step(state, inp):
        q_i, k_i, u_i, w_i, qk_i, g_i = inp
        v_new = u_i - jnp.einsum('bhcd,bhde->bhce', w_i, state)
        o = (jnp.einsum('bhcd,bhde->bhce', q_i * jnp.exp(g_i)[..., None], state)
             + jnp.einsum('bhij,bhje->bhie', qk_i, v_new))
        g_last = g_i[..., -1]
        state = (state * jnp.exp(g_last)[..., None, None]
                 + jnp.einsum('bhcd,bhce->bhde', k_i * jnp.exp(g_last[..., None] - g_i)[..., None], v_new))
        return state, o

    xs = tuple(jnp.moveaxis(t, 2, 0) for t in (q, k, u, w, qk, gcum))
    state0 = jnp.zeros((B, H, dk, dv), jnp.float32)
    _, o = lax.scan(step, state0, xs)
    o = jnp.transpose(o, (1, 0, 3, 2, 4)).reshape(B, S, H, dv)
    return o


def deltanet_mixer(h, w_in, conv_w, a_log, dt_bias, o_norm, w_out):
    B, S, _ = h.shape
    W = DN_WIDTH
    qkv, z, a, b = jnp.split(h @ w_in, [3 * W, 4 * W, 4 * W + DN_HEADS], axis=-1)
    qkv = jax.nn.silu(causal_dwconv(qkv, conv_w)).astype(jnp.float32)
    q, k, v = jnp.split(qkv, 3, axis=-1)
    shp = (B, S, DN_HEADS, DN_HEAD_DIM)
    q, k, v = l2norm(q.reshape(shp)), l2norm(k.reshape(shp)), v.reshape(shp)
    beta = jax.nn.sigmoid(b.astype(jnp.float32))
    g = -jnp.exp(a_log.astype(jnp.float32)) * jax.nn.softplus(a.astype(jnp.float32) + dt_bias.astype(jnp.float32))
    o = gated_delta_rule_chunked(q, k, v, g, beta)
    o = rmsnorm(o, o_norm) * jax.nn.silu(z.reshape(shp).astype(jnp.float32))
    return o.reshape(B, S, W).astype(h.dtype) @ w_out


def cross_attention(h, mem_h, wq, wkv, wo):
    B, S, _ = h.shape
    M = mem_h.shape[1]
    q = (h @ wq).reshape(B, S, XA_HEADS, XA_HEAD_DIM)
    k, v = jnp.split(mem_h @ wkv, 2, axis=-1)
    k = k.reshape(B, M, XA_HEADS, XA_HEAD_DIM)
    v = v.reshape(B, M, XA_HEADS, XA_HEAD_DIM)
    s = jnp.einsum('bshd,bmhd->bhsm', q, k).astype(jnp.float32) * (XA_HEAD_DIM ** -0.5)
    p = jax.nn.softmax(s, axis=-1).astype(v.dtype)
    o = jnp.einsum('bhsm,bmhd->bshd', p, v).reshape(B, S, D_MODEL)
    return o @ wo


def setup_inputs(seed: int = 0) -> dict:
    key = jax.random.key(seed)
    keys = iter(jax.random.split(key, 64))
    n_even = (DEPTH + 1) // 2
    n_odd = DEPTH // 2
    f32 = jnp.float32

    def nrm(shape, fan_in):
        return jax.random.normal(next(keys), shape, f32) * (fan_in ** -0.5)

    def gain(shape):
        return 1.0 + 0.02 * jax.random.normal(next(keys), shape, f32)

    def small(shape):
        return 0.01 * jax.random.normal(next(keys), shape, f32)

    x = jax.random.normal(next(keys), (BATCH, SEQ, D_MODEL), f32)
    mem = jax.random.normal(next(keys), (BATCH, N_MEM, D_MODEL), f32)
    blk = LRU_WIDTH // LRU_BLOCKS
    a0 = jax.random.uniform(next(keys), (n_even, LRU_WIDTH), f32, 0.9, 0.999)
    a_base = a0 ** (1.0 / LRU_C)
    lam = jnp.log(a_base) - jnp.log1p(-a_base)
    dt = jnp.exp(jax.random.uniform(next(keys), (n_odd, DN_HEADS), f32, math.log(1e-3), math.log(0.1)))
    dt_bias = dt + jnp.log(-jnp.expm1(-dt))
    a_log = jnp.log(jax.random.uniform(next(keys), (n_odd, DN_HEADS), f32, 1.0, 16.0))
    return {
        "x": x,
        "mem": mem,
        "ffn1_norm": gain((DEPTH, D_MODEL)),
        "ffn1_w_in": nrm((DEPTH, D_MODEL, 2 * D_FF), D_MODEL),
        "ffn1_w_out": nrm((DEPTH, D_FF, D_MODEL), D_FF),
        "mix_norm": gain((DEPTH, D_MODEL)),
        "xa_norm": gain((DEPTH, D_MODEL)),
        "xa_mem_norm": gain((DEPTH, D_MODEL)),
        "xa_wq": nrm((DEPTH, D_MODEL, D_MODEL), D_MODEL),
        "xa_wkv": nrm((DEPTH, D_MODEL, 2 * D_MODEL), D_MODEL),
        "xa_wo": nrm((DEPTH, D_MODEL, D_MODEL), D_MODEL),
        "ffn2_norm": gain((DEPTH, D_MODEL)),
        "ffn2_w_in": nrm((DEPTH, D_MODEL, 2 * D_FF), D_MODEL),
        "ffn2_w_out": nrm((DEPTH, D_FF, D_MODEL), D_FF),
        "ab_w_in": nrm((n_even, D_MODEL, AB_IN), D_MODEL),
        "lru_conv_w": nrm((n_even, CONV_K, LRU_WIDTH), CONV_K),
        "lru_conv_b": small((n_even, LRU_WIDTH)),
        "lru_w_a": nrm((n_even, LRU_BLOCKS, blk, blk), blk),
        "lru_b_a": small((n_even, LRU_WIDTH)),
        "lru_w_x": nrm((n_even, LRU_BLOCKS, blk, blk), blk),
        "lru_b_x": small((n_even, LRU_WIDTH)),
        "lru_lambda": lam,
        "ab_w_out": nrm((n_even, ATTN_WIDTH + LRU_WIDTH, D_MODEL), ATTN_WIDTH + LRU_WIDTH),
        "dn_w_in": nrm((n_odd, D_MODEL, DN_IN), D_MODEL),
        "dn_conv_w": nrm((n_odd, CONV_K, 3 * DN_WIDTH), CONV_K),
        "dn_a_log": a_log,
        "dn_dt_bias": dt_bias,
        "dn_o_norm": gain((n_odd, DN_HEAD_DIM)),
        "dn_w_out": nrm((n_odd, DN_WIDTH, D_MODEL), DN_WIDTH),
        "final_norm": gain((D_MODEL,)),
    }


def reference(x, mem, ffn1_norm, ffn1_w_in, ffn1_w_out, mix_norm, xa_norm, xa_mem_norm,
              xa_wq, xa_wkv, xa_wo, ffn2_norm, ffn2_w_in, ffn2_w_out,
              ab_w_in, lru_conv_w, lru_conv_b, lru_w_a, lru_b_a, lru_w_x, lru_b_x, lru_lambda, ab_w_out,
              dn_w_in, dn_conv_w, dn_a_log, dn_dt_bias, dn_o_norm, dn_w_out, final_norm):
    for layer in range(DEPTH):
        x = x + 0.5 * swiglu(rmsnorm(x, ffn1_norm[layer]), ffn1_w_in[layer], ffn1_w_out[layer])
        h = rmsnorm(x, mix_norm[layer])
        j = layer // 2
        if layer % 2 == 0:
            x = x + attn_lru_mixer(h, ab_w_in[j], lru_conv_w[j], lru_conv_b[j], lru_w_a[j], lru_b_a[j],
                                   lru_w_x[j], lru_b_x[j], lru_lambda[j], ab_w_out[j])
        else:
            x = x + deltanet_mixer(h, dn_w_in[j], dn_conv_w[j], dn_a_log[j], dn_dt_bias[j],
                                   dn_o_norm[j], dn_w_out[j])
        x = x + cross_attention(rmsnorm(x, xa_norm[layer]), rmsnorm(mem, xa_mem_norm[layer]),
                                xa_wq[layer], xa_wkv[layer], xa_wo[layer])
        x = x + 0.5 * swiglu(rmsnorm(x, ffn2_norm[layer]), ffn2_w_in[layer], ffn2_w_out[layer])
    return rmsnorm(x, final_norm)
```

```python
import functools
import math

import jax
import jax.numpy as jnp
from jax import lax
from jax.experimental import pallas as pl
from jax.experimental.pallas import tpu as pltpu

F32 = jnp.float32
BF16 = jnp.bfloat16

NORM_EPS = 1e-6
L2_EPS = 1e-6
LANES = 128
SUBLANES = 8

D_MODEL = 1024
D_FF = 2816
FF_CHUNK = 1408
ATTN_HEADS = 8
ATTN_HEAD_DIM = 64
ATTN_WIDTH = 512
ATTN_PAIRS = ATTN_WIDTH // LANES
DILATIONS = (1, 4, 16)
N_BACK = 128
ATTN_BLOCK = 128
ATTN_TILE = 2048
LRU_WIDTH = 512
LRU_C = 8.0
CONV_K = 4
CONV_PAD = SUBLANES
DN_HEADS = 8
DN_HEAD_DIM = 128
DN_WIDTH = 1024
DN_CHUNK = 128
DN_INV_BLOCK = 16
XA_HEADS = 4
XA_HEAD_DIM = 256
NEG_BIG = -1e30
VMEM_LIMIT = 56 * 1024 * 1024


def _cparams(sem):
    return pltpu.CompilerParams(dimension_semantics=sem, vmem_limit_bytes=VMEM_LIMIT)


def _resident(shape):
    nd = len(shape)
    return pl.BlockSpec(shape, lambda *_: (0,) * nd, pipeline_mode=pl.Buffered(1))


def _dot(a, b):
    return jnp.dot(a, b, preferred_element_type=F32)


def _dot_nt(a, b):
    return lax.dot_general(a, b, (((1,), (1,)), ((), ())), preferred_element_type=F32)


def _dot_tn(a, b):
    return lax.dot_general(a, b, (((0,), (0,)), ((), ())), preferred_element_type=F32)


def _rms(x, g):
    return x * lax.rsqrt(jnp.mean(x * x, axis=-1, keepdims=True) + NORM_EPS) * g


def _sigmoid(x):
    return 1.0 / (1.0 + jnp.exp(-x))


def _silu(x):
    return x * _sigmoid(x)


def _softplus(x):
    return jnp.maximum(x, 0.0) + jnp.log1p(jnp.exp(-jnp.abs(x)))


def _gelu_tanh(x):
    c = math.sqrt(2.0 / math.pi)
    return 0.5 * x * (1.0 + jnp.tanh(c * (x + 0.044715 * (x * x * x))))


def _ffn_body(x_ref, g_ref, win_ref, wout_ref, fg_ref, o_ref, *, final_norm):
    x = x_ref[...]
    h = _rms(x, g_ref[...]).astype(BF16)
    acc = jnp.zeros_like(x)
    for c in range(D_FF // FF_CHUNK):
        lo = c * FF_CHUNK
        gate = _dot(h, win_ref[:, lo:lo + FF_CHUNK])
        up = _dot(h, win_ref[:, D_FF + lo:D_FF + lo + FF_CHUNK])
        a = (_silu(gate) * up).astype(BF16)
        acc = acc + _dot(a, wout_ref[lo:lo + FF_CHUNK, :])
    y = x + 0.5 * acc
    if final_norm:
        y = _rms(y, fg_ref[...])
    o_ref[...] = y


def _ffn(x, g, w_in, w_out, fg, *, final_norm, tm=512):
    s = x.shape[0]
    row = pl.BlockSpec((tm, D_MODEL), lambda i: (i, 0))
    return pl.pallas_call(
        functools.partial(_ffn_body, final_norm=final_norm),
        out_shape=jax.ShapeDtypeStruct((s, D_MODEL), F32),
        grid=(s // tm,),
        in_specs=[row, _resident((1, D_MODEL)), _resident((D_MODEL, 2 * D_FF)),
                  _resident((D_FF, D_MODEL)), _resident((1, D_MODEL))],
        out_specs=row,
        compiler_params=_cparams(("arbitrary",)),
        name="ffn_final" if final_norm else "ffn",
    )(x, g, w_in, w_out, fg)


def _outproj_body(x_ref, *refs, n_in):
    y_refs, w_ref, o_ref = refs[:n_in], refs[n_in], refs[n_in + 1]
    parts = []
    for r in y_refs:
        if len(r.shape) == 3:
            parts.extend(r[j] for j in range(r.shape[0]))
        else:
            parts.append(r[...])
    y = parts[0] if len(parts) == 1 else jnp.concatenate(parts, axis=1)
    o_ref[...] = x_ref[...] + _dot(y, w_ref[...])


def _outproj(x, ys, w, *, tm=1024):
    s = x.shape[0]
    row = pl.BlockSpec((tm, D_MODEL), lambda i: (i, 0))
    y_specs = []
    for y in ys:
        if y.ndim == 3:
            y_specs.append(pl.BlockSpec((y.shape[0], tm, y.shape[2]), lambda i: (0, i, 0)))
        else:
            y_specs.append(pl.BlockSpec((tm, y.shape[1]), lambda i: (i, 0)))
    return pl.pallas_call(
        functools.partial(_outproj_body, n_in=len(ys)),
        out_shape=jax.ShapeDtypeStruct((s, D_MODEL), F32),
        grid=(s // tm,),
        in_specs=[row] + y_specs + [_resident(w.shape)],
        out_specs=row,
        compiler_params=_cparams(("arbitrary",)),
        name="outproj",
    )(x, *ys, w)


def _proj0_body(x_ref, g_ref, w_ref, q_ref, k_ref, v_ref, xr_ref, gr_ref):
    h = _rms(x_ref[...], g_ref[...]).astype(BF16)
    p = _dot(h, w_ref[...])
    scale = ATTN_HEAD_DIM ** -0.5
    for j in range(ATTN_PAIRS):
        q_ref[j] = p[:, j * LANES:(j + 1) * LANES] * scale
        k_ref[j] = p[:, ATTN_WIDTH + j * LANES:ATTN_WIDTH + (j + 1) * LANES]
        v_ref[j] = p[:, 2 * ATTN_WIDTH + j * LANES:2 * ATTN_WIDTH + (j + 1) * LANES]
    xr_ref[...] = p[:, 3 * ATTN_WIDTH:3 * ATTN_WIDTH + LRU_WIDTH]
    gr_ref[...] = p[:, 3 * ATTN_WIDTH + LRU_WIDTH:]


def _proj0(x, g, w, *, tm=512):
    s = x.shape[0]
    row = pl.BlockSpec((tm, D_MODEL), lambda i: (i, 0))
    pair = pl.BlockSpec((ATTN_PAIRS, tm, LANES), lambda i: (0, i, 0))
    half = pl.BlockSpec((tm, LRU_WIDTH), lambda i: (i, 0))
    pair_shape = jax.ShapeDtypeStruct((ATTN_PAIRS, s, LANES), F32)
    half_shape = jax.ShapeDtypeStruct((s, LRU_WIDTH), F32)
    return pl.pallas_call(
        _proj0_body,
        out_shape=(pair_shape, pair_shape, pair_shape, half_shape, half_shape),
        grid=(s // tm,),
        in_specs=[row, _resident((1, D_MODEL)), _resident(w.shape)],
        out_specs=(pair, pair, pair, half, half),
        compiler_params=_cparams(("arbitrary",)),
        name="proj0",
    )(x, g, w)


def _attn_body(q_ref, k_ref, v_ref, o_ref, kwin, vwin, og_ref, lg_ref):
    t = pl.program_id(1)
    hist = ATTN_TILE

    @pl.when(t == 0)
    def _():
        kwin[0:hist] = jnp.zeros((hist, LANES), F32)
        vwin[0:hist] = jnp.zeros((hist, LANES), F32)

    @pl.when(t > 0)
    def _():
        kwin[0:hist] = kwin[hist:2 * hist]
        vwin[0:hist] = vwin[hist:2 * hist]

    kwin[hist:2 * hist] = k_ref[0]
    vwin[hist:2 * hist] = v_ref[0]

    nk = 2 * ATTN_BLOCK
    qi = lax.broadcasted_iota(jnp.int32, (ATTN_BLOCK, nk), 0)
    kj = lax.broadcasted_iota(jnp.int32, (ATTN_BLOCK, nk), 1)
    dist = qi + ATTN_BLOCK - kj
    band = (dist >= 0) & (dist <= N_BACK)
    first = band & (kj >= jnp.where(t == 0, ATTN_BLOCK, 0))
    lane_lo = lax.broadcasted_iota(jnp.int32, (ATTN_BLOCK, LANES), 1) < ATTN_HEAD_DIM

    for gi, d in enumerate(DILATIONS):
        for r in range(d):
            for sub in range(ATTN_TILE // d // ATTN_BLOCK):
                q_rows = pl.ds(r + d * ATTN_BLOCK * sub, ATTN_BLOCK, stride=d)
                k_rows = pl.ds(hist + r + d * ATTN_BLOCK * (sub - 1), nk, stride=d)
                q = q_ref[0, q_rows, :]
                kk = kwin[k_rows, :].astype(BF16)
                vv = vwin[k_rows, :].astype(BF16)
                valid = first if sub == 0 else band
                o_pair = None
                l_pair = None
                for hp in range(2):
                    mine = lane_lo if hp == 0 else jnp.logical_not(lane_lo)
                    qm = jnp.where(mine, q, 0.0).astype(BF16)
                    s = jnp.where(valid, _dot_nt(qm, kk), NEG_BIG)
                    m = jnp.max(s, axis=-1, keepdims=True)
                    p = jnp.exp(s - m)
                    den = jnp.sum(p, axis=-1, keepdims=True)
                    o = _dot(p.astype(BF16), vv) / den
                    lse = jnp.broadcast_to(m + jnp.log(den), (ATTN_BLOCK, LANES))
                    o_pair = o if hp == 0 else jnp.where(lane_lo, o_pair, o)
                    l_pair = lse if hp == 0 else jnp.where(lane_lo, l_pair, lse)
                og_ref[gi, q_rows, :] = o_pair
                lg_ref[gi, q_rows, :] = l_pair

    l0, l1, l2 = lg_ref[0], lg_ref[1], lg_ref[2]
    lm = jnp.maximum(jnp.maximum(l0, l1), l2)
    w0, w1, w2 = jnp.exp(l0 - lm), jnp.exp(l1 - lm), jnp.exp(l2 - lm)
    out = (w0 * og_ref[0] + w1 * og_ref[1] + w2 * og_ref[2]) / (w0 + w1 + w2)
    o_ref[0] = out.astype(BF16)


def _attention(q, k, v):
    s = q.shape[1]
    blk = pl.BlockSpec((1, ATTN_TILE, LANES), lambda p, t: (p, t, 0))
    return pl.pallas_call(
        _attn_body,
        out_shape=jax.ShapeDtypeStruct((ATTN_PAIRS, s, LANES), BF16),
        grid=(ATTN_PAIRS, s // ATTN_TILE),
        in_specs=[blk, blk, blk],
        out_specs=blk,
        scratch_shapes=[pltpu.VMEM((2 * ATTN_TILE, LANES), F32),
                        pltpu.VMEM((2 * ATTN_TILE, LANES), F32),
                        pltpu.VMEM((len(DILATIONS), ATTN_TILE, LANES), F32),
                        pltpu.VMEM((len(DILATIONS), ATTN_TILE, LANES), F32)],
        compiler_params=_cparams(("arbitrary", "arbitrary")),
        name="dilated_attn",
    )(q, k, v)


def _scan_step(a, b, k, pos):
    ok = pos >= k
    a_sh = pltpu.roll(a, k, 0)
    b_sh = pltpu.roll(b, k, 0)
    return jnp.where(ok, a * a_sh, a), jnp.where(ok, a * b_sh + b, b)


def _lru_body(xr_ref, gr_ref, cw_ref, cb_ref, wa_ref, ba_ref, wx_ref, bx_ref, lam_ref, y_ref,
              ext_ref, carry_ref, a_s, b_s, cin_s, *, tm):
    i = pl.program_id(0)

    @pl.when(i == 0)
    def _():
        ext_ref[0:CONV_PAD] = jnp.zeros((CONV_PAD, LRU_WIDTH), F32)
        carry_ref[...] = jnp.zeros((SUBLANES, LRU_WIDTH), F32)

    @pl.when(i > 0)
    def _():
        ext_ref[0:CONV_PAD] = ext_ref[tm:tm + CONV_PAD]

    ext_ref[CONV_PAD:CONV_PAD + tm] = xr_ref[...]
    xc = cb_ref[...]
    for k in range(CONV_K):
        xc = xc + cw_ref[k:k + 1, :] * ext_ref[pl.ds(CONV_PAD - (CONV_K - 1) + k, tm), :]
    xcb = xc.astype(BF16)
    r = _sigmoid(_dot(xcb, wa_ref[...]) + ba_ref[...])
    ig = _sigmoid(_dot(xcb, wx_ref[...]) + bx_ref[...])
    log_a = -LRU_C * r * _softplus(-lam_ref[...])
    a = jnp.exp(log_a)
    b = jnp.sqrt(-jnp.tanh(log_a) * (a * a + 1.0)) * ig * xc

    pos = lax.broadcasted_iota(jnp.int32, (tm, LRU_WIDTH), 0) & (SUBLANES - 1)
    for k in (1, 2, 4):
        a, b = _scan_step(a, b, k, pos)
    lane_groups = [slice(g * LANES, (g + 1) * LANES) for g in range(LRU_WIDTH // LANES)]
    for g, cols in enumerate(lane_groups):
        a_s[g] = a[:, cols]
        b_s[g] = b[:, cols]
    ng = tm // SUBLANES
    ends = pl.ds(SUBLANES - 1, ng, stride=SUBLANES)
    ae = jnp.concatenate([a_s[g, ends, :] for g in range(len(lane_groups))], axis=1)
    be = jnp.concatenate([b_s[g, ends, :] for g in range(len(lane_groups))], axis=1)
    gpos = lax.broadcasted_iota(jnp.int32, (ng, LRU_WIDTH), 0)
    k = 1
    while k < ng:
        ae, be = _scan_step(ae, be, k, gpos)
        k *= 2
    carry = carry_ref[0:1, :]
    h_end = ae * carry + be
    h_in = jnp.where(gpos >= 1, pltpu.roll(h_end, 1, 0), carry)
    for g, cols in enumerate(lane_groups):
        for j in range(SUBLANES):
            cin_s[g, pl.ds(j, ng, stride=SUBLANES), :] = h_in[:, cols]
    carry_ref[...] = jnp.broadcast_to(h_end[ng - 1:ng, :], (SUBLANES, LRU_WIDTH))
    h = jnp.concatenate([a_s[g] * cin_s[g] + b_s[g] for g in range(len(lane_groups))], axis=1)
    y_ref[...] = (h * _gelu_tanh(gr_ref[...])).astype(BF16)


def _lru(xr, gr, cw, cb, wa, ba, wx, bx, lam, *, tm=1024):
    s = xr.shape[0]
    row = pl.BlockSpec((tm, LRU_WIDTH), lambda i: (i, 0))
    vec = _resident((1, LRU_WIDTH))
    sq = _resident((LRU_WIDTH, LRU_WIDTH))
    return pl.pallas_call(
        functools.partial(_lru_body, tm=tm),
        out_shape=jax.ShapeDtypeStruct((s, LRU_WIDTH), BF16),
        grid=(s // tm,),
        in_specs=[row, row, _resident((CONV_K, LRU_WIDTH)), vec, sq, vec, sq, vec, vec],
        out_specs=row,
        scratch_shapes=[pltpu.VMEM((tm + CONV_PAD, LRU_WIDTH), F32),
                        pltpu.VMEM((SUBLANES, LRU_WIDTH), F32),
                        pltpu.VMEM((LRU_WIDTH // LANES, tm, LANES), F32),
                        pltpu.VMEM((LRU_WIDTH // LANES, tm, LANES), F32),
                        pltpu.VMEM((LRU_WIDTH // LANES, tm, LANES), F32)],
        compiler_params=_cparams(("arbitrary",)),
        name="rglru",
    )(xr, gr, cw, cb, wa, ba, wx, bx, lam)


def _memkv_body(m_ref, g_ref, w_ref, k_ref, v_ref):
    h = _rms(m_ref[...], g_ref[...]).astype(BF16)
    kv = _dot(h, w_ref[...])
    k_ref[...] = kv[:, :D_MODEL].astype(BF16)
    v_ref[...] = kv[:, D_MODEL:].astype(BF16)


def _memkv(mem, g, wkv):
    n = mem.shape[0]
    shp = jax.ShapeDtypeStruct((n, D_MODEL), BF16)
    return pl.pallas_call(
        _memkv_body, out_shape=(shp, shp),
        compiler_params=pltpu.CompilerParams(vmem_limit_bytes=VMEM_LIMIT),
        name="mem_kv",
    )(mem, g, wkv)


def _xattn_body(x_ref, g_ref, wq_ref, k_ref, v_ref, wo_ref, o_ref):
    x = x_ref[...]
    h = _rms(x, g_ref[...]).astype(BF16)
    q = (_dot(h, wq_ref[...]) * (XA_HEAD_DIM ** -0.5)).astype(BF16)
    outs = []
    for hh in range(XA_HEADS):
        cols = slice(hh * XA_HEAD_DIM, (hh + 1) * XA_HEAD_DIM)
        s = _dot_nt(q[:, cols], k_ref[:, cols])
        p = jnp.exp(s - jnp.max(s, axis=-1, keepdims=True))
        p = (p / jnp.sum(p, axis=-1, keepdims=True)).astype(BF16)
        outs.append(_dot(p, v_ref[:, cols]).astype(BF16))
    o_ref[...] = x + _dot(jnp.concatenate(outs, axis=1), wo_ref[...])


def _xattn(x, g, wq, k, v, wo, *, tm=512):
    s = x.shape[0]
    row = pl.BlockSpec((tm, D_MODEL), lambda i: (i, 0))
    return pl.pallas_call(
        _xattn_body,
        out_shape=jax.ShapeDtypeStruct((s, D_MODEL), F32),
        grid=(s // tm,),
        in_specs=[row, _resident((1, D_MODEL)), _resident(wq.shape), _resident(k.shape),
                  _resident(v.shape), _resident(wo.shape)],
        out_specs=row,
        compiler_params=_cparams(("arbitrary",)),
        name="cross_attn",
    )(x, g, wq, k, v, wo)


def _dnprep_body(x_ref, g_ref, wqkv_ref, wz_ref, wab_ref, wabt_ref, cw_ref,
                 q_ref, k_ref, v_ref, z_ref, ab_ref, abt_ref, ext_ref, *, tm):
    i = pl.program_id(0)
    width = 3 * DN_WIDTH

    @pl.when(i == 0)
    def _():
        ext_ref[0:CONV_PAD] = jnp.zeros((CONV_PAD, width), F32)

    @pl.when(i > 0)
    def _():
        ext_ref[0:CONV_PAD] = ext_ref[tm:tm + CONV_PAD]

    h = _rms(x_ref[...], g_ref[...]).astype(BF16)
    ext_ref[CONV_PAD:CONV_PAD + tm] = _dot(h, wqkv_ref[...])
    z_ref[...] = _dot(h, wz_ref[...])
    ab_ref[...] = _dot(h, wab_ref[...])
    abt = _dot_nt(wabt_ref[...], h)
    for j in range(tm // DN_CHUNK):
        abt_ref[j] = abt[:, j * DN_CHUNK:(j + 1) * DN_CHUNK]

    outs = (q_ref, k_ref, v_ref)
    for part in range(3):
        for hh in range(DN_HEADS):
            cols = slice(part * DN_WIDTH + hh * DN_HEAD_DIM, part * DN_WIDTH + (hh + 1) * DN_HEAD_DIM)
            acc = None
            for k in range(CONV_K):
                term = cw_ref[k:k + 1, cols] * ext_ref[pl.ds(CONV_PAD - (CONV_K - 1) + k, tm), cols]
                acc = term if acc is None else acc + term
            u = _silu(acc)
            if part < 2:
                u = u * lax.rsqrt(jnp.sum(u * u, axis=-1, keepdims=True) + L2_EPS)
            if part == 0:
                u = u * (DN_HEAD_DIM ** -0.5)
            outs[part][hh] = u.astype(BF16)


def _dnprep(x, g, wqkv, wz, wab, wabt, cw, *, tm=512):
    s = x.shape[0]
    row = pl.BlockSpec((tm, D_MODEL), lambda i: (i, 0))
    head = pl.BlockSpec((DN_HEADS, tm, DN_HEAD_DIM), lambda i: (0, i, 0))
    head_shape = jax.ShapeDtypeStruct((DN_HEADS, s, DN_HEAD_DIM), BF16)
    nab = wabt.shape[0]
    return pl.pallas_call(
        functools.partial(_dnprep_body, tm=tm),
        out_shape=(head_shape, head_shape, head_shape,
                   jax.ShapeDtypeStruct((s, DN_WIDTH), F32),
                   jax.ShapeDtypeStruct((s, LANES), F32),
                   jax.ShapeDtypeStruct((s // DN_CHUNK, nab, DN_CHUNK), F32)),
        grid=(s // tm,),
        in_specs=[row, _resident((1, D_MODEL)), _resident(wqkv.shape), _resident(wz.shape),
                  _resident(wab.shape), _resident(wabt.shape), _resident(cw.shape)],
        out_specs=(head, head, head, row,
                   pl.BlockSpec((tm, LANES), lambda i: (i, 0)),
                   pl.BlockSpec((tm // DN_CHUNK, nab, DN_CHUNK), lambda i: (i, 0, 0))),
        scratch_shapes=[pltpu.VMEM((tm + CONV_PAD, 3 * DN_WIDTH), F32)],
        compiler_params=_cparams(("arbitrary",)),
        name="dn_prep",
    )(x, g, wqkv, wz, wab, wabt, cw)


def _dn_body(q_ref, k_ref, v_ref, z_ref, ab_ref, abt_ref, prow_ref, pcol_ref, on_ref, y_ref, state, *, ts):
    c = DN_CHUNK

    @pl.when(pl.program_id(0) == 0)
    def _():
        state[...] = jnp.zeros((DN_HEADS, DN_HEAD_DIM, DN_HEAD_DIM), F32)

    ri = lax.broadcasted_iota(jnp.int32, (c, c), 0)
    ci = lax.broadcasted_iota(jnp.int32, (c, c), 1)
    lower = ri >= ci
    strict = ri > ci
    tri_l = lower.astype(F32)
    tri_u = (ri <= ci).astype(F32)
    eye = (ri == ci).astype(F32)
    diag_blk = (ri // DN_INV_BLOCK) == (ci // DN_INV_BLOCK)
    off_blks = []
    bsz = DN_INV_BLOCK
    while bsz < c:
        off_blks.append(((ri // (2 * bsz)) == (ci // (2 * bsz))) & ((ri // bsz) != (ci // bsz)))
        bsz *= 2
    hi = lax.Precision.HIGHEST
    a_log_row, dtb_row = prow_ref[0:1, :], prow_ref[1:2, :]
    a_log_col, dtb_col = pcol_ref[:, 0:1], pcol_ref[:, 1:2]
    onorm = on_ref[...]

    def chunk(ci_, _):
        rows = pl.ds(pl.multiple_of(ci_ * c, c), c)
        ab = ab_ref[rows, :]
        abt = abt_ref[ci_]
        g_col = -jnp.exp(a_log_row) * _softplus(ab + dtb_row)
        gc_col = jnp.dot(tri_l, g_col, precision=hi, preferred_element_type=F32)
        beta_all = _sigmoid(ab)
        g_row = -jnp.exp(a_log_col) * _softplus(abt + dtb_col)
        gc_row = jnp.dot(g_row, tri_u, precision=hi, preferred_element_type=F32)
        for hh in range(DN_HEADS):
            gcol = gc_col[:, hh:hh + 1]
            grow = gc_row[hh:hh + 1, :]
            beta = beta_all[:, DN_HEADS + hh:DN_HEADS + hh + 1]
            decay = jnp.where(lower, jnp.exp(jnp.minimum(gcol - grow, 0.0)), 0.0)
            egc = jnp.exp(gcol)
            glast = gcol[c - 1:c, :]
            kf = k_ref[hh, rows, :].astype(F32)
            kb = kf * beta
            kbb = kb.astype(BF16)
            kk = k_ref[hh, rows, :]
            qq = q_ref[hh, rows, :]
            vf = v_ref[hh, rows, :].astype(F32)
            m = jnp.where(strict, _dot_nt(kbb, kk) * decay, 0.0) * -1.0
            md = jnp.where(diag_blk, m, 0.0)
            t_inv = eye + md
            pw = md
            for _ in range(int(math.log2(DN_INV_BLOCK)) - 1):
                pwb = pw.astype(BF16)
                pw = _dot(pwb, pwb)
                t_inv = t_inv + _dot(t_inv.astype(BF16), pw.astype(BF16))
            for off_blk in off_blks:
                tb = t_inv.astype(BF16)
                m_off = jnp.where(off_blk, m, 0.0).astype(BF16)
                t_inv = t_inv + _dot(tb, _dot(m_off, tb).astype(BF16))
            rhs = jnp.concatenate([vf * beta, kb * egc], axis=1).astype(BF16)
            sol = _dot(t_inv.astype(BF16), rhs)
            u, w = sol[:, :DN_HEAD_DIM], sol[:, DN_HEAD_DIM:]
            qk = jnp.where(lower, _dot_nt(qq, kk) * decay, 0.0)
            st = state[hh]
            stb = st.astype(BF16)
            v_new = u - _dot(w.astype(BF16), stb)
            vnb = v_new.astype(BF16)
            o = _dot((qq.astype(F32) * egc).astype(BF16), stb) + _dot(qk.astype(BF16), vnb)
            kd = (kf * jnp.exp(glast - gcol)).astype(BF16)
            state[hh] = st * jnp.exp(glast) + _dot_tn(kd, vnb)
            cols = slice(hh * DN_HEAD_DIM, (hh + 1) * DN_HEAD_DIM)
            on = o * lax.rsqrt(jnp.mean(o * o, axis=-1, keepdims=True) + NORM_EPS) * onorm
            y_ref[rows, cols] = (on * _silu(z_ref[rows, cols])).astype(BF16)
        return 0

    lax.fori_loop(0, ts // c, chunk, 0)


def _deltanet(q, k, v, z, ab, abt, prow, pcol, onorm, *, ts=1024):
    s = z.shape[0]
    head = pl.BlockSpec((DN_HEADS, ts, DN_HEAD_DIM), lambda i: (0, i, 0))
    nab = abt.shape[1]
    return pl.pallas_call(
        functools.partial(_dn_body, ts=ts),
        out_shape=jax.ShapeDtypeStruct((s, DN_WIDTH), BF16),
        grid=(s // ts,),
        in_specs=[head, head, head,
                  pl.BlockSpec((ts, DN_WIDTH), lambda i: (i, 0)),
                  pl.BlockSpec((ts, LANES), lambda i: (i, 0)),
                  pl.BlockSpec((ts // DN_CHUNK, nab, DN_CHUNK), lambda i: (i, 0, 0)),
                  _resident(prow.shape), _resident(pcol.shape), _resident(onorm.shape)],
        out_specs=pl.BlockSpec((ts, DN_WIDTH), lambda i: (i, 0)),
        scratch_shapes=[pltpu.VMEM((DN_HEADS, DN_HEAD_DIM, DN_HEAD_DIM), F32)],
        compiler_params=_cparams(("arbitrary",)),
        name="gated_deltanet",
    )(q, k, v, z, ab, abt, prow, pcol, onorm)


def _row(v):
    return v.reshape(1, -1).astype(F32)


def _block_diag(w):
    n, j, k = w.shape
    eye = jnp.eye(n, dtype=w.dtype)
    return (w[:, :, None, :] * eye[:, None, :, None]).reshape(n * j, n * k)


def kernel(x, mem, ffn1_norm, ffn1_w_in, ffn1_w_out, mix_norm, xa_norm, xa_mem_norm, xa_wq, xa_wkv, xa_wo,
           ffn2_norm, ffn2_w_in, ffn2_w_out, ab_w_in, lru_conv_w, lru_conv_b, lru_w_a, lru_b_a, lru_w_x,
           lru_b_x, lru_lambda, ab_w_out, dn_w_in, dn_conv_w, dn_a_log, dn_dt_bias, dn_o_norm, dn_w_out,
           final_norm):
    batch, seq, _ = x.shape
    depth = ffn1_norm.shape[0]
    outs = []
    for bi in range(batch):
        xs = x[bi]
        ms = mem[bi]
        for layer in range(depth):
            j = layer // 2
            last = layer == depth - 1
            xs = _ffn(xs, _row(ffn1_norm[layer]), ffn1_w_in[layer].astype(BF16),
                      ffn1_w_out[layer].astype(BF16), _row(final_norm), final_norm=False)
            if layer % 2 == 0:
                q, k, v, xr, gr = _proj0(xs, _row(mix_norm[layer]), ab_w_in[j].astype(BF16))
                attn = _attention(q, k, v)
                y = _lru(xr, gr, lru_conv_w[j].astype(F32), _row(lru_conv_b[j]),
                         _block_diag(lru_w_a[j]).astype(BF16), _row(lru_b_a[j]),
                         _block_diag(lru_w_x[j]).astype(BF16), _row(lru_b_x[j]), _row(lru_lambda[j]))
                xs = _outproj(xs, (attn, y), ab_w_out[j].astype(BF16))
            else:
                w = dn_w_in[j]
                wab = w[:, 4 * DN_WIDTH:]
                n_ab = wab.shape[1]
                wab_pad = jnp.pad(wab, ((0, 0), (0, LANES - n_ab))).astype(BF16)
                q, k, v, z, ab, abt = _dnprep(
                    xs, _row(mix_norm[layer]), w[:, :3 * DN_WIDTH].astype(BF16),
                    w[:, 3 * DN_WIDTH:4 * DN_WIDTH].astype(BF16), wab_pad, wab.T.astype(BF16),
                    dn_conv_w[j].astype(F32))
                pad = jnp.zeros((LANES - DN_HEADS,), F32)
                prow = jnp.stack([jnp.concatenate([dn_a_log[j].astype(F32), pad]),
                                  jnp.concatenate([dn_dt_bias[j].astype(F32), pad])])
                pad2 = jnp.zeros((n_ab - DN_HEADS,), F32)
                pcol = jnp.stack([jnp.concatenate([dn_a_log[j].astype(F32), pad2]),
                                  jnp.concatenate([dn_dt_bias[j].astype(F32), pad2])], axis=1)
                y = _deltanet(q, k, v, z, ab, abt, prow, pcol, _row(dn_o_norm[j]))
                xs = _outproj(xs, (y,), dn_w_out[j].astype(BF16))
            mk, mv = _memkv(ms, _row(xa_mem_norm[layer]), xa_wkv[layer].astype(BF16))
            xs = _xattn(xs, _row(xa_norm[layer]), xa_wq[layer].astype(BF16), mk, mv,
                        xa_wo[layer].astype(BF16))
            xs = _ffn(xs, _row(ffn2_norm[layer]), ffn2_w_in[layer].astype(BF16),
                      ffn2_w_out[layer].astype(BF16), _row(final_norm), final_norm=last)
        outs.append(xs)
    return jnp.stack(outs, axis=0)
```

```python
import functools
import math

import jax
import jax.numpy as jnp
from jax import lax
from jax.experimental import pallas as pl
from jax.experimental.pallas import tpu as pltpu

F32 = jnp.float32
BF16 = jnp.bfloat16

NORM_EPS = 1e-6
L2_EPS = 1e-6
LANES = 128
SUBLANES = 8

D_MODEL = 1024
D_FF = 2816
MXU_DIM = 256
FF_CHUNKS = (6 * MXU_DIM, 5 * MXU_DIM)
assert sum(FF_CHUNKS) == D_FF
ATTN_HEADS = 8
ATTN_HEAD_DIM = 64
ATTN_WIDTH = 512
ATTN_PAIRS = ATTN_WIDTH // LANES
DILATIONS = (1, 4, 16)
N_BACK = 128
ATTN_BLOCK = 128
ATTN_TILE = 2048
LRU_WIDTH = 512
LRU_C = 8.0
CONV_K = 4
CONV_PAD = SUBLANES
DN_HEADS = 8
DN_HEAD_DIM = 128
DN_WIDTH = 1024
DN_CHUNK = 128
DN_INV_BLOCK = 16
XA_HEADS = 4
XA_HEAD_DIM = 256
NEG_BIG = -1e30
VMEM_LIMIT = 56 * 1024 * 1024


def _cparams(sem):
    return pltpu.CompilerParams(dimension_semantics=sem, vmem_limit_bytes=VMEM_LIMIT)


def _resident(shape, layer=None):
    if layer is None:
        nd = len(shape)
        return pl.BlockSpec(shape, lambda *_: (0,) * nd, pipeline_mode=pl.Buffered(1))
    nd = len(shape) - 1
    return pl.BlockSpec((None,) + tuple(shape[1:]), lambda *_: (layer,) + (0,) * nd,
                        pipeline_mode=pl.Buffered(1))


def _dot(a, b):
    return jnp.dot(a, b, preferred_element_type=F32)


def _dot_nt(a, b):
    return lax.dot_general(a, b, (((1,), (1,)), ((), ())), preferred_element_type=F32)


def _dot_tn(a, b):
    return lax.dot_general(a, b, (((0,), (0,)), ((), ())), preferred_element_type=F32)


def _rms(x, g):
    return x * lax.rsqrt(jnp.mean(x * x, axis=-1, keepdims=True) + NORM_EPS) * g


def _sigmoid(x):
    return 1.0 / (1.0 + jnp.exp(-x))


def _silu(x):
    return x * _sigmoid(x)


def _softplus(x):
    return jnp.maximum(x, 0.0) + jnp.log1p(jnp.exp(-jnp.abs(x)))


def _causal_conv(ext, w, tm):
    y = w[CONV_K - 1:CONV_K, :] * ext[CONV_PAD:CONV_PAD + tm]
    for shift in range(1, CONV_K):
        k = CONV_K - 1 - shift
        y = y + w[k:k + 1, :] * pltpu.roll(ext, shift, 0)[CONV_PAD:CONV_PAD + tm]
    return y


def _gelu_tanh(x):
    c = math.sqrt(2.0 / math.pi)
    return 0.5 * x * (1.0 + jnp.tanh(c * (x + 0.044715 * (x * x * x))))


def _ffn_body(x_ref, g_ref, win_ref, wout_ref, fg_ref, o_ref, *, final_norm):
    x = x_ref[...]
    h = _rms(x, g_ref[...]).astype(BF16)
    acc = jnp.zeros_like(x)
    lo = 0
    for width in FF_CHUNKS:
        gate = _dot(h, win_ref[:, lo:lo + width])
        up = _dot(h, win_ref[:, D_FF + lo:D_FF + lo + width])
        a = (_silu(gate) * up).astype(BF16)
        acc = acc + _dot(a, wout_ref[lo:lo + width, :])
        lo += width
    y = x + 0.5 * acc
    if final_norm:
        y = _rms(y, fg_ref[...])
    o_ref[...] = y


def _ffn(x, g, w_in, w_out, fg, layer, *, final_norm, tm=512):
    s = x.shape[0]
    row = pl.BlockSpec((tm, D_MODEL), lambda i: (i, 0))
    return pl.pallas_call(
        functools.partial(_ffn_body, final_norm=final_norm),
        out_shape=jax.ShapeDtypeStruct((s, D_MODEL), F32),
        grid=(s // tm,),
        in_specs=[row, _resident((1, D_MODEL)), _resident(w_in.shape, layer),
                  _resident(w_out.shape, layer), _resident((1, D_MODEL))],
        out_specs=row,
        compiler_params=_cparams(("arbitrary",)),
        name="ffn_final" if final_norm else "ffn",
    )(x, g, w_in, w_out, fg)


def _outproj_body(x_ref, *refs, n_in):
    y_refs, w_ref, o_ref = refs[:n_in], refs[n_in], refs[n_in + 1]
    parts = []
    for r in y_refs:
        if len(r.shape) == 3:
            parts.extend(r[j] for j in range(r.shape[0]))
        else:
            parts.append(r[...])
    y = parts[0] if len(parts) == 1 else jnp.concatenate(parts, axis=1)
    o_ref[...] = x_ref[...] + _dot(y, w_ref[...])


def _outproj(x, ys, w, *, tm=1024):
    s = x.shape[0]
    row = pl.BlockSpec((tm, D_MODEL), lambda i: (i, 0))
    y_specs = []
    for y in ys:
        if y.ndim == 3:
            y_specs.append(pl.BlockSpec((y.shape[0], tm, y.shape[2]), lambda i: (0, i, 0)))
        else:
            y_specs.append(pl.BlockSpec((tm, y.shape[1]), lambda i: (i, 0)))
    return pl.pallas_call(
        functools.partial(_outproj_body, n_in=len(ys)),
        out_shape=jax.ShapeDtypeStruct((s, D_MODEL), F32),
        grid=(s // tm,),
        in_specs=[row] + y_specs + [_resident(w.shape)],
        out_specs=row,
        compiler_params=_cparams(("arbitrary",)),
        name="outproj",
    )(x, *ys, w)


def _proj0_body(x_ref, g_ref, w_ref, q_ref, k_ref, v_ref, xr_ref, gr_ref):
    h = _rms(x_ref[...], g_ref[...]).astype(BF16)
    p = _dot(h, w_ref[...])
    scale = ATTN_HEAD_DIM ** -0.5
    for j in range(ATTN_PAIRS):
        q_ref[j] = p[:, j * LANES:(j + 1) * LANES] * scale
        k_ref[j] = p[:, ATTN_WIDTH + j * LANES:ATTN_WIDTH + (j + 1) * LANES]
        v_ref[j] = p[:, 2 * ATTN_WIDTH + j * LANES:2 * ATTN_WIDTH + (j + 1) * LANES]
    xr_ref[...] = p[:, 3 * ATTN_WIDTH:3 * ATTN_WIDTH + LRU_WIDTH]
    gr_ref[...] = p[:, 3 * ATTN_WIDTH + LRU_WIDTH:]


def _proj0(x, g, w, *, tm=512):
    s = x.shape[0]
    row = pl.BlockSpec((tm, D_MODEL), lambda i: (i, 0))
    pair = pl.BlockSpec((ATTN_PAIRS, tm, LANES), lambda i: (0, i, 0))
    half = pl.BlockSpec((tm, LRU_WIDTH), lambda i: (i, 0))
    pair_shape = jax.ShapeDtypeStruct((ATTN_PAIRS, s, LANES), F32)
    half_shape = jax.ShapeDtypeStruct((s, LRU_WIDTH), F32)
    return pl.pallas_call(
        _proj0_body,
        out_shape=(pair_shape, pair_shape, pair_shape, half_shape, half_shape),
        grid=(s // tm,),
        in_specs=[row, _resident((1, D_MODEL)), _resident(w.shape)],
        out_specs=(pair, pair, pair, half, half),
        compiler_params=_cparams(("arbitrary",)),
        name="proj0",
    )(x, g, w)


def _attn_body(q_ref, k_ref, v_ref, o_ref, kwin, vwin, og_ref, lg_ref):
    t = pl.program_id(1)
    hist = ATTN_TILE

    @pl.when(t == 0)
    def _():
        kwin[0:hist] = jnp.zeros((hist, LANES), F32)
        vwin[0:hist] = jnp.zeros((hist, LANES), F32)

    @pl.when(t > 0)
    def _():
        kwin[0:hist] = kwin[hist:2 * hist]
        vwin[0:hist] = vwin[hist:2 * hist]

    kwin[hist:2 * hist] = k_ref[0]
    vwin[hist:2 * hist] = v_ref[0]

    nk = 2 * ATTN_BLOCK
    qi = lax.broadcasted_iota(jnp.int32, (ATTN_BLOCK, nk), 0)
    kj = lax.broadcasted_iota(jnp.int32, (ATTN_BLOCK, nk), 1)
    dist = qi + ATTN_BLOCK - kj
    band = (dist >= 0) & (dist <= N_BACK)
    first = band & (kj >= jnp.where(t == 0, ATTN_BLOCK, 0))
    lane_lo = lax.broadcasted_iota(jnp.int32, (ATTN_BLOCK, LANES), 1) < ATTN_HEAD_DIM

    for gi, d in enumerate(DILATIONS):
        for r in range(d):
            for sub in range(ATTN_TILE // d // ATTN_BLOCK):
                q_rows = pl.ds(r + d * ATTN_BLOCK * sub, ATTN_BLOCK, stride=d)
                k_rows = pl.ds(hist + r + d * ATTN_BLOCK * (sub - 1), nk, stride=d)
                q = q_ref[0, q_rows, :]
                kk = kwin[k_rows, :].astype(BF16)
                vv = vwin[k_rows, :].astype(BF16)
                valid = first if sub == 0 else band
                o_pair = None
                l_pair = None
                for hp in range(2):
                    mine = lane_lo if hp == 0 else jnp.logical_not(lane_lo)
                    qm = jnp.where(mine, q, 0.0).astype(BF16)
                    s = jnp.where(valid, _dot_nt(qm, kk), NEG_BIG)
                    m = jnp.max(s, axis=-1, keepdims=True)
                    p = jnp.exp(s - m)
                    den = jnp.sum(p, axis=-1, keepdims=True)
                    o = _dot(p.astype(BF16), vv) / den
                    lse = jnp.broadcast_to(m + jnp.log(den), (ATTN_BLOCK, LANES))
                    o_pair = o if hp == 0 else jnp.where(lane_lo, o_pair, o)
                    l_pair = lse if hp == 0 else jnp.where(lane_lo, l_pair, lse)
                og_ref[gi, q_rows, :] = o_pair
                lg_ref[gi, q_rows, :] = l_pair

    l0, l1, l2 = lg_ref[0], lg_ref[1], lg_ref[2]
    lm = jnp.maximum(jnp.maximum(l0, l1), l2)
    w0, w1, w2 = jnp.exp(l0 - lm), jnp.exp(l1 - lm), jnp.exp(l2 - lm)
    out = (w0 * og_ref[0] + w1 * og_ref[1] + w2 * og_ref[2]) / (w0 + w1 + w2)
    o_ref[0] = out.astype(BF16)


def _attention(q, k, v):
    s = q.shape[1]
    blk = pl.BlockSpec((1, ATTN_TILE, LANES), lambda p, t: (p, t, 0))
    return pl.pallas_call(
        _attn_body,
        out_shape=jax.ShapeDtypeStruct((ATTN_PAIRS, s, LANES), BF16),
        grid=(ATTN_PAIRS, s // ATTN_TILE),
        in_specs=[blk, blk, blk],
        out_specs=blk,
        scratch_shapes=[pltpu.VMEM((2 * ATTN_TILE, LANES), F32),
                        pltpu.VMEM((2 * ATTN_TILE, LANES), F32),
                        pltpu.VMEM((len(DILATIONS), ATTN_TILE, LANES), F32),
                        pltpu.VMEM((len(DILATIONS), ATTN_TILE, LANES), F32)],
        compiler_params=_cparams(("arbitrary", "arbitrary")),
        name="dilated_attn",
    )(q, k, v)


def _scan_step(a, b, k, pos):
    ok = pos >= k
    a_sh = pltpu.roll(a, k, 0)
    b_sh = pltpu.roll(b, k, 0)
    return jnp.where(ok, a * a_sh, a), jnp.where(ok, a * b_sh + b, b)


def _lru_body(xr_ref, gr_ref, cw_ref, cb_ref, wa_ref, ba_ref, wx_ref, bx_ref, lam_ref, y_ref,
              ext_ref, carry_ref, a_s, b_s, cin_s, *, tm):
    i = pl.program_id(0)

    @pl.when(i == 0)
    def _():
        ext_ref[0:CONV_PAD] = jnp.zeros((CONV_PAD, LRU_WIDTH), F32)
        carry_ref[...] = jnp.zeros((SUBLANES, LRU_WIDTH), F32)

    @pl.when(i > 0)
    def _():
        ext_ref[0:CONV_PAD] = ext_ref[tm:tm + CONV_PAD]

    ext_ref[CONV_PAD:CONV_PAD + tm] = xr_ref[...]
    xc = _causal_conv(ext_ref[...], cw_ref[...], tm) + cb_ref[...]
    xcb = xc.astype(BF16)
    r = _sigmoid(_dot(xcb, wa_ref[...]) + ba_ref[...])
    ig = _sigmoid(_dot(xcb, wx_ref[...]) + bx_ref[...])
    log_a = -LRU_C * r * _softplus(-lam_ref[...])
    a = jnp.exp(log_a)
    b = jnp.sqrt(-jnp.tanh(log_a) * (a * a + 1.0)) * ig * xc

    pos = lax.broadcasted_iota(jnp.int32, (tm, LRU_WIDTH), 0) & (SUBLANES - 1)
    for k in (1, 2, 4):
        a, b = _scan_step(a, b, k, pos)
    lane_groups = [slice(g * LANES, (g + 1) * LANES) for g in range(LRU_WIDTH // LANES)]
    for g, cols in enumerate(lane_groups):
        a_s[g] = a[:, cols]
        b_s[g] = b[:, cols]
    ng = tm // SUBLANES
    ends = pl.ds(SUBLANES - 1, ng, stride=SUBLANES)
    ae = jnp.concatenate([a_s[g, ends, :] for g in range(len(lane_groups))], axis=1)
    be = jnp.concatenate([b_s[g, ends, :] for g in range(len(lane_groups))], axis=1)
    gpos = lax.broadcasted_iota(jnp.int32, (ng, LRU_WIDTH), 0)
    k = 1
    while k < ng:
        ae, be = _scan_step(ae, be, k, gpos)
        k *= 2
    carry = carry_ref[0:1, :]
    h_end = ae * carry + be
    h_in = jnp.where(gpos >= 1, pltpu.roll(h_end, 1, 0), carry)
    for g, cols in enumerate(lane_groups):
        for j in range(SUBLANES):
            cin_s[g, pl.ds(j, ng, stride=SUBLANES), :] = h_in[:, cols]
    carry_ref[...] = jnp.broadcast_to(h_end[ng - 1:ng, :], (SUBLANES, LRU_WIDTH))
    h = jnp.concatenate([a_s[g] * cin_s[g] + b_s[g] for g in range(len(lane_groups))], axis=1)
    y_ref[...] = (h * _gelu_tanh(gr_ref[...])).astype(BF16)


def _lru(xr, gr, cw, cb, wa, ba, wx, bx, lam, *, tm=1024):
    s = xr.shape[0]
    row = pl.BlockSpec((tm, LRU_WIDTH), lambda i: (i, 0))
    vec = _resident((1, LRU_WIDTH))
    sq = _resident((LRU_WIDTH, LRU_WIDTH))
    return pl.pallas_call(
        functools.partial(_lru_body, tm=tm),
        out_shape=jax.ShapeDtypeStruct((s, LRU_WIDTH), BF16),
        grid=(s // tm,),
        in_specs=[row, row, _resident((CONV_K, LRU_WIDTH)), vec, sq, vec, sq, vec, vec],
        out_specs=row,
        scratch_shapes=[pltpu.VMEM((tm + CONV_PAD, LRU_WIDTH), F32),
                        pltpu.VMEM((SUBLANES, LRU_WIDTH), F32),
                        pltpu.VMEM((LRU_WIDTH // LANES, tm, LANES), F32),
                        pltpu.VMEM((LRU_WIDTH // LANES, tm, LANES), F32),
                        pltpu.VMEM((LRU_WIDTH // LANES, tm, LANES), F32)],
        compiler_params=_cparams(("arbitrary",)),
        name="rglru",
    )(xr, gr, cw, cb, wa, ba, wx, bx, lam)


def _memkv_body(m_ref, g_ref, w_ref, k_ref, v_ref):
    h = _rms(m_ref[...], g_ref[...]).astype(BF16)
    kv = _dot(h, w_ref[...])
    k_ref[...] = kv[:, :D_MODEL].astype(BF16)
    v_ref[...] = kv[:, D_MODEL:].astype(BF16)


def _memkv(mem, g, wkv):
    n = mem.shape[0]
    shp = jax.ShapeDtypeStruct((n, D_MODEL), BF16)
    return pl.pallas_call(
        _memkv_body, out_shape=(shp, shp),
        compiler_params=pltpu.CompilerParams(vmem_limit_bytes=VMEM_LIMIT),
        name="mem_kv",
    )(mem, g, wkv)


def _xattn_body(x_ref, g_ref, wq_ref, k_ref, v_ref, wo_ref, o_ref):
    x = x_ref[...]
    h = _rms(x, g_ref[...]).astype(BF16)
    q = (_dot(h, wq_ref[...]) * (XA_HEAD_DIM ** -0.5)).astype(BF16)
    outs = []
    for hh in range(XA_HEADS):
        cols = slice(hh * XA_HEAD_DIM, (hh + 1) * XA_HEAD_DIM)
        s = _dot_nt(q[:, cols], k_ref[:, cols])
        p = jnp.exp(s - jnp.max(s, axis=-1, keepdims=True))
        p = (p / jnp.sum(p, axis=-1, keepdims=True)).astype(BF16)
        outs.append(_dot(p, v_ref[:, cols]).astype(BF16))
    o_ref[...] = x + _dot(jnp.concatenate(outs, axis=1), wo_ref[...])


def _xattn(x, g, wq, k, v, wo, layer, *, tm=512):
    s = x.shape[0]
    row = pl.BlockSpec((tm, D_MODEL), lambda i: (i, 0))
    return pl.pallas_call(
        _xattn_body,
        out_shape=jax.ShapeDtypeStruct((s, D_MODEL), F32),
        grid=(s // tm,),
        in_specs=[row, _resident((1, D_MODEL)), _resident(wq.shape, layer), _resident(k.shape),
                  _resident(v.shape), _resident(wo.shape, layer)],
        out_specs=row,
        compiler_params=_cparams(("arbitrary",)),
        name="cross_attn",
    )(x, g, wq, k, v, wo)


def _dnprep_body(x_ref, g_ref, w_ref, wab_ref, wabt_ref, cw_ref,
                 q_ref, k_ref, v_ref, z_ref, ab_ref, abt_ref, ext_ref, *, tm):
    i = pl.program_id(0)
    width = 3 * DN_WIDTH

    @pl.when(i == 0)
    def _():
        ext_ref[0:CONV_PAD] = jnp.zeros((CONV_PAD, width), F32)

    @pl.when(i > 0)
    def _():
        ext_ref[0:CONV_PAD] = ext_ref[tm:tm + CONV_PAD]

    h = _rms(x_ref[...], g_ref[...]).astype(BF16)
    ext_ref[CONV_PAD:CONV_PAD + tm] = _dot(h, w_ref[:, :width])
    z_ref[...] = _dot(h, w_ref[:, width:width + DN_WIDTH])
    ab_ref[...] = _dot(h, wab_ref[...])
    abt = _dot_nt(wabt_ref[...], h)
    for j in range(tm // DN_CHUNK):
        abt_ref[j] = abt[:, j * DN_CHUNK:(j + 1) * DN_CHUNK]

    outs = (q_ref, k_ref, v_ref)
    for part in range(3):
        for hh in range(DN_HEADS):
            cols = slice(part * DN_WIDTH + hh * DN_HEAD_DIM, part * DN_WIDTH + (hh + 1) * DN_HEAD_DIM)
            u = _silu(_causal_conv(ext_ref[:, cols], cw_ref[:, cols], tm))
            if part < 2:
                u = u * lax.rsqrt(jnp.sum(u * u, axis=-1, keepdims=True) + L2_EPS)
            if part == 0:
                u = u * (DN_HEAD_DIM ** -0.5)
            outs[part][hh] = u.astype(BF16)


def _dnprep(x, g, w, wab, wabt, cw, *, tm=512):
    s = x.shape[0]
    row = pl.BlockSpec((tm, D_MODEL), lambda i: (i, 0))
    head = pl.BlockSpec((DN_HEADS, tm, DN_HEAD_DIM), lambda i: (0, i, 0))
    head_shape = jax.ShapeDtypeStruct((DN_HEADS, s, DN_HEAD_DIM), BF16)
    nab = wabt.shape[0]
    return pl.pallas_call(
        functools.partial(_dnprep_body, tm=tm),
        out_shape=(head_shape, head_shape, head_shape,
                   jax.ShapeDtypeStruct((s, DN_WIDTH), F32),
                   jax.ShapeDtypeStruct((s, LANES), F32),
                   jax.ShapeDtypeStruct((s // DN_CHUNK, nab, DN_CHUNK), F32)),
        grid=(s // tm,),
        in_specs=[row, _resident((1, D_MODEL)), _resident(w.shape),
                  _resident(wab.shape), _resident(wabt.shape), _resident(cw.shape)],
        out_specs=(head, head, head, row,
                   pl.BlockSpec((tm, LANES), lambda i: (i, 0)),
                   pl.BlockSpec((tm // DN_CHUNK, nab, DN_CHUNK), lambda i: (i, 0, 0))),
        scratch_shapes=[pltpu.VMEM((tm + CONV_PAD, 3 * DN_WIDTH), F32)],
        compiler_params=_cparams(("arbitrary",)),
        name="dn_prep",
    )(x, g, w, wab, wabt, cw)


def _dn_body(q_ref, k_ref, v_ref, z_ref, ab_ref, abt_ref, prow_ref, pcol_ref, on_ref, y_ref, state, *, ts):
    c = DN_CHUNK

    @pl.when(pl.program_id(0) == 0)
    def _():
        state[...] = jnp.zeros((DN_HEADS, DN_HEAD_DIM, DN_HEAD_DIM), F32)

    ri = lax.broadcasted_iota(jnp.int32, (c, c), 0)
    ci = lax.broadcasted_iota(jnp.int32, (c, c), 1)
    lower = ri >= ci
    strict = ri > ci
    tri_l = lower.astype(F32)
    tri_u = (ri <= ci).astype(F32)
    eye = (ri == ci).astype(F32)
    diag_blk = (ri // DN_INV_BLOCK) == (ci // DN_INV_BLOCK)
    off_blks = []
    bsz = DN_INV_BLOCK
    while bsz < c:
        off_blks.append(((ri // (2 * bsz)) == (ci // (2 * bsz))) & ((ri // bsz) != (ci // bsz)))
        bsz *= 2
    hi = lax.Precision.HIGHEST
    a_log_row, dtb_row = prow_ref[0:1, :], prow_ref[1:2, :]
    a_log_col, dtb_col = pcol_ref[:, 0:1], pcol_ref[:, 1:2]
    onorm = on_ref[...]

    def chunk(ci_, _):
        rows = pl.ds(pl.multiple_of(ci_ * c, c), c)
        ab = ab_ref[rows, :]
        abt = abt_ref[ci_]
        g_col = -jnp.exp(a_log_row) * _softplus(ab + dtb_row)
        gc_col = jnp.dot(tri_l, g_col, precision=hi, preferred_element_type=F32)
        beta_all = _sigmoid(ab)
        g_row = -jnp.exp(a_log_col) * _softplus(abt + dtb_col)
        gc_row = jnp.dot(g_row, tri_u, precision=hi, preferred_element_type=F32)
        heads = range(DN_HEADS)
        gcol = [gc_col[:, h:h + 1] for h in heads]
        beta = [beta_all[:, DN_HEADS + h:DN_HEADS + h + 1] for h in heads]
        decay = [jnp.where(lower, jnp.exp(jnp.minimum(gcol[h] - gc_row[h:h + 1, :], 0.0)), 0.0) for h in heads]
        egc = [jnp.exp(gcol[h]) for h in heads]
        glast = [gcol[h][c - 1:c, :] for h in heads]
        kk = [k_ref[h, rows, :] for h in heads]
        qq = [q_ref[h, rows, :] for h in heads]
        kf = [kk[h].astype(F32) for h in heads]
        kb = [kf[h] * beta[h] for h in heads]
        m = [jnp.where(strict, _dot_nt(kb[h].astype(BF16), kk[h]) * decay[h], 0.0) * -1.0 for h in heads]
        qk = [jnp.where(lower, _dot_nt(qq[h], kk[h]) * decay[h], 0.0).astype(BF16) for h in heads]
        pw = [jnp.where(diag_blk, m[h], 0.0) for h in heads]
        t_inv = [eye + pw[h] for h in heads]
        for _ in range(int(math.log2(DN_INV_BLOCK)) - 1):
            pwb = [pw[h].astype(BF16) for h in heads]
            pw = [_dot(pwb[h], pwb[h]) for h in heads]
            t_inv = [t_inv[h] + _dot(t_inv[h].astype(BF16), pw[h].astype(BF16)) for h in heads]
        for off_blk in off_blks:
            tb = [t_inv[h].astype(BF16) for h in heads]
            x = [_dot(jnp.where(off_blk, m[h], 0.0).astype(BF16), tb[h]).astype(BF16) for h in heads]
            t_inv = [t_inv[h] + _dot(tb[h], x[h]) for h in heads]
        rhs = [jnp.concatenate([v_ref[h, rows, :].astype(F32) * beta[h], kb[h] * egc[h]], axis=1).astype(BF16)
               for h in heads]
        sol = [_dot(t_inv[h].astype(BF16), rhs[h]) for h in heads]
        st = [state[h] for h in heads]
        stb = [st[h].astype(BF16) for h in heads]
        v_new = [sol[h][:, :DN_HEAD_DIM] - _dot(sol[h][:, DN_HEAD_DIM:].astype(BF16), stb[h]) for h in heads]
        vnb = [v_new[h].astype(BF16) for h in heads]
        o = [_dot((qq[h].astype(F32) * egc[h]).astype(BF16), stb[h]) + _dot(qk[h], vnb[h]) for h in heads]
        for h in heads:
            kd = (kf[h] * jnp.exp(glast[h] - gcol[h])).astype(BF16)
            state[h] = st[h] * jnp.exp(glast[h]) + _dot_tn(kd, vnb[h])
        for h in heads:
            cols = slice(h * DN_HEAD_DIM, (h + 1) * DN_HEAD_DIM)
            on = o[h] * lax.rsqrt(jnp.mean(o[h] * o[h], axis=-1, keepdims=True) + NORM_EPS) * onorm
            y_ref[rows, cols] = (on * _silu(z_ref[rows, cols])).astype(BF16)
        return 0

    lax.fori_loop(0, ts // c, chunk, 0)


def _deltanet(q, k, v, z, ab, abt, prow, pcol, onorm, *, ts=1024):
    s = z.shape[0]
    head = pl.BlockSpec((DN_HEADS, ts, DN_HEAD_DIM), lambda i: (0, i, 0))
    nab = abt.shape[1]
    return pl.pallas_call(
        functools.partial(_dn_body, ts=ts),
        out_shape=jax.ShapeDtypeStruct((s, DN_WIDTH), BF16),
        grid=(s // ts,),
        in_specs=[head, head, head,
                  pl.BlockSpec((ts, DN_WIDTH), lambda i: (i, 0)),
                  pl.BlockSpec((ts, LANES), lambda i: (i, 0)),
                  pl.BlockSpec((ts // DN_CHUNK, nab, DN_CHUNK), lambda i: (i, 0, 0)),
                  _resident(prow.shape), _resident(pcol.shape), _resident(onorm.shape)],
        out_specs=pl.BlockSpec((ts, DN_WIDTH), lambda i: (i, 0)),
        scratch_shapes=[pltpu.VMEM((DN_HEADS, DN_HEAD_DIM, DN_HEAD_DIM), F32)],
        compiler_params=_cparams(("arbitrary",)),
        name="gated_deltanet",
    )(q, k, v, z, ab, abt, prow, pcol, onorm)


def _row(v):
    return v.reshape(1, -1).astype(F32)


def _block_diag(w):
    n, j, k = w.shape
    eye = jnp.eye(n, dtype=w.dtype)
    return (w[:, :, None, :] * eye[:, None, :, None]).reshape(n * j, n * k)


def kernel(x, mem, ffn1_norm, ffn1_w_in, ffn1_w_out, mix_norm, xa_norm, xa_mem_norm, xa_wq, xa_wkv, xa_wo,
           ffn2_norm, ffn2_w_in, ffn2_w_out, ab_w_in, lru_conv_w, lru_conv_b, lru_w_a, lru_b_a, lru_w_x,
           lru_b_x, lru_lambda, ab_w_out, dn_w_in, dn_conv_w, dn_a_log, dn_dt_bias, dn_o_norm, dn_w_out,
           final_norm):
    batch, seq, _ = x.shape
    depth = ffn1_norm.shape[0]
    ffn1_w_in, ffn1_w_out, ffn2_w_in, ffn2_w_out, xa_wq, xa_wo = (
        w.astype(BF16) for w in (ffn1_w_in, ffn1_w_out, ffn2_w_in, ffn2_w_out, xa_wq, xa_wo))
    outs = []
    for bi in range(batch):
        xs = x[bi]
        ms = mem[bi]
        for layer in range(depth):
            j = layer // 2
            last = layer == depth - 1
            xs = _ffn(xs, _row(ffn1_norm[layer]), ffn1_w_in, ffn1_w_out, _row(final_norm), layer,
                      final_norm=False)
            if layer % 2 == 0:
                q, k, v, xr, gr = _proj0(xs, _row(mix_norm[layer]), ab_w_in[j].astype(BF16))
                attn = _attention(q, k, v)
                y = _lru(xr, gr, lru_conv_w[j].astype(F32), _row(lru_conv_b[j]),
                         _block_diag(lru_w_a[j]).astype(BF16), _row(lru_b_a[j]),
                         _block_diag(lru_w_x[j]).astype(BF16), _row(lru_b_x[j]), _row(lru_lambda[j]))
                xs = _outproj(xs, (attn, y), ab_w_out[j].astype(BF16))
            else:
                w = dn_w_in[j]
                wab = w[:, 4 * DN_WIDTH:]
                n_ab = wab.shape[1]
                wab_pad = jnp.pad(wab, ((0, 0), (0, LANES - n_ab))).astype(BF16)
                q, k, v, z, ab, abt = _dnprep(
                    xs, _row(mix_norm[layer]), w.astype(BF16), wab_pad, wab.T.astype(BF16),
                    dn_conv_w[j].astype(F32))
                pad = jnp.zeros((LANES - DN_HEADS,), F32)
                prow = jnp.stack([jnp.concatenate([dn_a_log[j].astype(F32), pad]),
                                  jnp.concatenate([dn_dt_bias[j].astype(F32), pad])])
                pad2 = jnp.zeros((n_ab - DN_HEADS,), F32)
                pcol = jnp.stack([jnp.concatenate([dn_a_log[j].astype(F32), pad2]),
                                  jnp.concatenate([dn_dt_bias[j].astype(F32), pad2])], axis=1)
                y = _deltanet(q, k, v, z, ab, abt, prow, pcol, _row(dn_o_norm[j]))
                xs = _outproj(xs, (y,), dn_w_out[j].astype(BF16))
            mk, mv = _memkv(ms, _row(xa_mem_norm[layer]), xa_wkv[layer].astype(BF16))
            xs = _xattn(xs, _row(xa_norm[layer]), xa_wq, mk, mv, xa_wo, layer)
            xs = _ffn(xs, _row(ffn2_norm[layer]), ffn2_w_in, ffn2_w_out, _row(final_norm), layer,
                      final_norm=last)
        outs.append(xs)
    return jnp.stack(outs, axis=0)
```

```python
import functools
import math

import jax
import jax.numpy as jnp
from jax import lax
from jax.experimental import pallas as pl
from jax.experimental.pallas import tpu as pltpu

F32 = jnp.float32
BF16 = jnp.bfloat16

NORM_EPS = 1e-6
L2_EPS = 1e-6
LANES = 128
SUBLANES = 8

D_MODEL = 1024
D_FF = 2816
MXU_DIM = 256
FF_CHUNKS = (6 * MXU_DIM, 5 * MXU_DIM)
assert sum(FF_CHUNKS) == D_FF
ATTN_HEADS = 8
ATTN_HEAD_DIM = 64
ATTN_WIDTH = 512
ATTN_PAIRS = ATTN_WIDTH // LANES
DILATIONS = (1, 4, 16)
N_BACK = 128
ATTN_BLOCK = 128
ATTN_TILE = 2048
LRU_WIDTH = 512
LRU_C = 8.0
CONV_K = 4
CONV_PAD = SUBLANES
DN_HEADS = 8
DN_HEAD_DIM = 128
DN_WIDTH = 1024
DN_CHUNK = 128
DN_INV_BLOCK = 16
DN_UNROLL = 2
DN_PREP_ROWS = 64
XA_HEADS = 4
XA_HEAD_DIM = 256
NEG_BIG = -1e30
VMEM_LIMIT = 56 * 1024 * 1024


def _cparams(sem):
    return pltpu.CompilerParams(dimension_semantics=sem, vmem_limit_bytes=VMEM_LIMIT)


def _resident(shape, layer=None):
    if layer is None:
        nd = len(shape)
        return pl.BlockSpec(shape, lambda *_: (0,) * nd, pipeline_mode=pl.Buffered(1))
    nd = len(shape) - 1
    return pl.BlockSpec((None,) + tuple(shape[1:]), lambda *_: (layer,) + (0,) * nd,
                        pipeline_mode=pl.Buffered(1))


def _dot(a, b):
    return jnp.dot(a, b, preferred_element_type=F32)


def _dot_nt(a, b):
    return lax.dot_general(a, b, (((1,), (1,)), ((), ())), preferred_element_type=F32)


def _dot_tn(a, b):
    return lax.dot_general(a, b, (((0,), (0,)), ((), ())), preferred_element_type=F32)


def _rms(x, g):
    return x * lax.rsqrt(jnp.mean(x * x, axis=-1, keepdims=True) + NORM_EPS) * g


def _sigmoid(x):
    return 1.0 / (1.0 + jnp.exp(-x))


def _silu(x):
    return x * _sigmoid(x)


def _softplus(x):
    return jnp.maximum(x, 0.0) + jnp.log1p(jnp.exp(-jnp.abs(x)))


def _causal_conv(ext, w, tm):
    y = w[CONV_K - 1:CONV_K, :] * ext[CONV_PAD:CONV_PAD + tm]
    for shift in range(1, CONV_K):
        k = CONV_K - 1 - shift
        y = y + w[k:k + 1, :] * pltpu.roll(ext, shift, 0)[CONV_PAD:CONV_PAD + tm]
    return y


def _gelu_tanh(x):
    c = math.sqrt(2.0 / math.pi)
    return 0.5 * x * (1.0 + jnp.tanh(c * (x + 0.044715 * (x * x * x))))


def _ffn_body(x_ref, g_ref, win_ref, wout_ref, fg_ref, o_ref, *, final_norm):
    x = x_ref[...]
    h = _rms(x, g_ref[...]).astype(BF16)
    acc = jnp.zeros_like(x)
    lo = 0
    for width in FF_CHUNKS:
        gate = _dot(h, win_ref[:, lo:lo + width])
        up = _dot(h, win_ref[:, D_FF + lo:D_FF + lo + width])
        a = (_silu(gate) * up).astype(BF16)
        acc = acc + _dot(a, wout_ref[lo:lo + width, :])
        lo += width
    y = x + 0.5 * acc
    if final_norm:
        y = _rms(y, fg_ref[...])
    o_ref[...] = y


def _ffn(x, g, w_in, w_out, fg, layer, *, final_norm, tm=512):
    s = x.shape[0]
    row = pl.BlockSpec((tm, D_MODEL), lambda i: (i, 0))
    return pl.pallas_call(
        functools.partial(_ffn_body, final_norm=final_norm),
        out_shape=jax.ShapeDtypeStruct((s, D_MODEL), F32),
        grid=(s // tm,),
        in_specs=[row, _resident((1, D_MODEL)), _resident(w_in.shape, layer),
                  _resident(w_out.shape, layer), _resident((1, D_MODEL))],
        out_specs=row,
        compiler_params=_cparams(("arbitrary",)),
        name="ffn_final" if final_norm else "ffn",
    )(x, g, w_in, w_out, fg)


def _proj0_body(x_ref, g_ref, w_ref, q_ref, k_ref, v_ref, xr_ref, gr_ref):
    h = _rms(x_ref[...], g_ref[...]).astype(BF16)
    p = _dot(h, w_ref[...])
    scale = ATTN_HEAD_DIM ** -0.5
    for j in range(ATTN_PAIRS):
        q_ref[j] = (p[:, j * LANES:(j + 1) * LANES] * scale).astype(BF16)
        k_ref[j] = p[:, ATTN_WIDTH + j * LANES:ATTN_WIDTH + (j + 1) * LANES].astype(BF16)
        v_ref[j] = p[:, 2 * ATTN_WIDTH + j * LANES:2 * ATTN_WIDTH + (j + 1) * LANES].astype(BF16)
    xr_ref[...] = p[:, 3 * ATTN_WIDTH:3 * ATTN_WIDTH + LRU_WIDTH]
    gr_ref[...] = p[:, 3 * ATTN_WIDTH + LRU_WIDTH:]


def _proj0(x, g, w, *, tm=512):
    s = x.shape[0]
    row = pl.BlockSpec((tm, D_MODEL), lambda i: (i, 0))
    pair = pl.BlockSpec((ATTN_PAIRS, tm, LANES), lambda i: (0, i, 0))
    half = pl.BlockSpec((tm, LRU_WIDTH), lambda i: (i, 0))
    pair_shape = jax.ShapeDtypeStruct((ATTN_PAIRS, s, LANES), BF16)
    half_shape = jax.ShapeDtypeStruct((s, LRU_WIDTH), F32)
    return pl.pallas_call(
        _proj0_body,
        out_shape=(pair_shape, pair_shape, pair_shape, half_shape, half_shape),
        grid=(s // tm,),
        in_specs=[row, _resident((1, D_MODEL)), _resident(w.shape)],
        out_specs=(pair, pair, pair, half, half),
        compiler_params=_cparams(("arbitrary",)),
        name="proj0",
    )(x, g, w)


def _attn_body(*refs):
    nd = len(DILATIONS)
    q_refs, k_refs, v_refs = refs[0:nd], refs[nd:2 * nd], refs[2 * nd:3 * nd]
    o_ref = refs[3 * nd]
    khist, vhist = refs[3 * nd + 1:4 * nd + 1], refs[4 * nd + 1:5 * nd + 1]
    og_ref, lg_ref = refs[5 * nd + 1], refs[5 * nd + 2]
    t = pl.program_id(1)

    @pl.when(t == 0)
    def _():
        for h_ref in khist + vhist:
            h_ref[...] = jnp.zeros(h_ref.shape, BF16)

    nk = N_BACK + ATTN_BLOCK
    qi = lax.broadcasted_iota(jnp.int32, (ATTN_BLOCK, nk), 0)
    kj = lax.broadcasted_iota(jnp.int32, (ATTN_BLOCK, nk), 1)
    dist = qi + N_BACK - kj
    band = (dist >= 0) & (dist <= N_BACK)
    bias = jnp.where(band, 0.0, NEG_BIG)
    bias_first = jnp.where(band & (kj >= jnp.where(t == 0, N_BACK, 0)), 0.0, NEG_BIG)
    lane = lax.broadcasted_iota(jnp.int32, (1, LANES), 1)
    head_mask = [(lane < ATTN_HEAD_DIM).astype(BF16), (lane >= ATTN_HEAD_DIM).astype(BF16)]
    lane_lo = lax.broadcasted_iota(jnp.int32, (ATTN_BLOCK, LANES), 1) < ATTN_HEAD_DIM

    for gi, d in enumerate(DILATIONS):
        q_ref, k_ref, v_ref = q_refs[gi], k_refs[gi], v_refs[gi]
        for r in range(d):
            lanes = slice(r * LANES, (r + 1) * LANES)
            for sub in range(ATTN_TILE // d // ATTN_BLOCK):
                lo = ATTN_BLOCK * sub
                q = q_ref[0, lo:lo + ATTN_BLOCK, lanes]
                if sub == 0:
                    kk = jnp.concatenate([khist[gi][:, lanes], k_ref[0, 0:ATTN_BLOCK, lanes]], axis=0)
                    vv = jnp.concatenate([vhist[gi][:, lanes], v_ref[0, 0:ATTN_BLOCK, lanes]], axis=0)
                else:
                    kk = k_ref[0, lo - N_BACK:lo + ATTN_BLOCK, lanes]
                    vv = v_ref[0, lo - N_BACK:lo + ATTN_BLOCK, lanes]
                o_pair = None
                l_pair = None
                for hp in range(2):
                    s = _dot_nt(q * head_mask[hp], kk) + (bias_first if sub == 0 else bias)
                    m = jnp.max(s, axis=-1, keepdims=True)
                    p = jnp.exp(s - m)
                    den = jnp.sum(p, axis=-1, keepdims=True)
                    o = _dot(p.astype(BF16), vv) / den
                    lse = jnp.broadcast_to(m + jnp.log(den), (ATTN_BLOCK, LANES))
                    o_pair = o if hp == 0 else jnp.where(lane_lo, o_pair, o)
                    l_pair = lse if hp == 0 else jnp.where(lane_lo, l_pair, lse)
                tok_rows = pl.ds(r + d * lo, ATTN_BLOCK, stride=d)
                og_ref[gi, tok_rows, :] = o_pair
                lg_ref[gi, tok_rows, :] = l_pair
        rows = ATTN_TILE // d
        khist[gi][...] = k_ref[0, rows - N_BACK:rows, :]
        vhist[gi][...] = v_ref[0, rows - N_BACK:rows, :]

    l0, l1, l2 = lg_ref[0], lg_ref[1], lg_ref[2]
    lm = jnp.maximum(jnp.maximum(l0, l1), l2)
    w0, w1, w2 = jnp.exp(l0 - lm), jnp.exp(l1 - lm), jnp.exp(l2 - lm)
    out = (w0 * og_ref[0] + w1 * og_ref[1] + w2 * og_ref[2]) / (w0 + w1 + w2)
    o_ref[0] = out.astype(BF16)


def _attention(q, k, v):
    s = q.shape[1]
    views, specs = [], []
    for arr in (q, k, v):
        for d in DILATIONS:
            views.append(arr.reshape(ATTN_PAIRS, s // d, d * LANES))
            specs.append(pl.BlockSpec((1, ATTN_TILE // d, d * LANES), lambda p, t: (p, t, 0)))
    hist = [pltpu.VMEM((N_BACK, d * LANES), BF16) for d in DILATIONS]
    return pl.pallas_call(
        _attn_body,
        out_shape=jax.ShapeDtypeStruct((ATTN_PAIRS, s, LANES), BF16),
        grid=(ATTN_PAIRS, s // ATTN_TILE),
        in_specs=specs,
        out_specs=pl.BlockSpec((1, ATTN_TILE, LANES), lambda p, t: (p, t, 0)),
        scratch_shapes=hist + hist + [pltpu.VMEM((len(DILATIONS), ATTN_TILE, LANES), F32),
                                      pltpu.VMEM((len(DILATIONS), ATTN_TILE, LANES), F32)],
        compiler_params=_cparams(("arbitrary", "arbitrary")),
        name="dilated_attn",
    )(*views)


def _scan_step(a, b, k, pos):
    ok = pos >= k
    a_sh = pltpu.roll(a, k, 0)
    b_sh = pltpu.roll(b, k, 0)
    return jnp.where(ok, a * a_sh, a), jnp.where(ok, a * b_sh + b, b)


def _lru_body(xr_ref, gr_ref, cw_ref, cb_ref, wa_ref, ba_ref, wx_ref, bx_ref, lam_ref, y_ref,
              ext_ref, carry_ref, a_s, b_s, cin_s, *, tm):
    i = pl.program_id(0)

    @pl.when(i == 0)
    def _():
        ext_ref[0:CONV_PAD] = jnp.zeros((CONV_PAD, LRU_WIDTH), F32)
        carry_ref[...] = jnp.zeros((SUBLANES, LRU_WIDTH), F32)

    @pl.when(i > 0)
    def _():
        ext_ref[0:CONV_PAD] = ext_ref[tm:tm + CONV_PAD]

    ext_ref[CONV_PAD:CONV_PAD + tm] = xr_ref[...]
    xc = _causal_conv(ext_ref[...], cw_ref[...], tm) + cb_ref[...]
    xcb = xc.astype(BF16)
    r = _sigmoid(_dot(xcb, wa_ref[...]) + ba_ref[...])
    ig = _sigmoid(_dot(xcb, wx_ref[...]) + bx_ref[...])
    log_a = -LRU_C * r * _softplus(-lam_ref[...])
    a = jnp.exp(log_a)
    b = jnp.sqrt(-jnp.tanh(log_a) * (a * a + 1.0)) * ig * xc

    pos = lax.broadcasted_iota(jnp.int32, (tm, LRU_WIDTH), 0) & (SUBLANES - 1)
    for k in (1, 2, 4):
        a, b = _scan_step(a, b, k, pos)
    lane_groups = [slice(g * LANES, (g + 1) * LANES) for g in range(LRU_WIDTH // LANES)]
    for g, cols in enumerate(lane_groups):
        a_s[g] = a[:, cols]
        b_s[g] = b[:, cols]
    ng = tm // SUBLANES
    ends = pl.ds(SUBLANES - 1, ng, stride=SUBLANES)
    ae = jnp.concatenate([a_s[g, ends, :] for g in range(len(lane_groups))], axis=1)
    be = jnp.concatenate([b_s[g, ends, :] for g in range(len(lane_groups))], axis=1)
    gpos = lax.broadcasted_iota(jnp.int32, (ng, LRU_WIDTH), 0)
    k = 1
    while k < ng:
        ae, be = _scan_step(ae, be, k, gpos)
        k *= 2
    carry = carry_ref[0:1, :]
    h_end = ae * carry + be
    h_in = jnp.where(gpos >= 1, pltpu.roll(h_end, 1, 0), carry)
    for g, cols in enumerate(lane_groups):
        for j in range(SUBLANES):
            cin_s[g, pl.ds(j, ng, stride=SUBLANES), :] = h_in[:, cols]
    carry_ref[...] = jnp.broadcast_to(h_end[ng - 1:ng, :], (SUBLANES, LRU_WIDTH))
    h = jnp.concatenate([a_s[g] * cin_s[g] + b_s[g] for g in range(len(lane_groups))], axis=1)
    y_ref[...] = (h * _gelu_tanh(gr_ref[...])).astype(BF16)


def _lru(xr, gr, cw, cb, wa, ba, wx, bx, lam, *, tm=1024):
    s = xr.shape[0]
    row = pl.BlockSpec((tm, LRU_WIDTH), lambda i: (i, 0))
    vec = _resident((1, LRU_WIDTH))
    sq = _resident((LRU_WIDTH, LRU_WIDTH))
    return pl.pallas_call(
        functools.partial(_lru_body, tm=tm),
        out_shape=jax.ShapeDtypeStruct((s, LRU_WIDTH), BF16),
        grid=(s // tm,),
        in_specs=[row, row, _resident((CONV_K, LRU_WIDTH)), vec, sq, vec, sq, vec, vec],
        out_specs=row,
        scratch_shapes=[pltpu.VMEM((tm + CONV_PAD, LRU_WIDTH), F32),
                        pltpu.VMEM((SUBLANES, LRU_WIDTH), F32),
                        pltpu.VMEM((LRU_WIDTH // LANES, tm, LANES), F32),
                        pltpu.VMEM((LRU_WIDTH // LANES, tm, LANES), F32),
                        pltpu.VMEM((LRU_WIDTH // LANES, tm, LANES), F32)],
        compiler_params=_cparams(("arbitrary",)),
        name="rglru",
    )(xr, gr, cw, cb, wa, ba, wx, bx, lam)


def _memkv_body(m_ref, g_ref, w_ref, k_ref, v_ref):
    h = _rms(m_ref[...], g_ref[...]).astype(BF16)
    kv = _dot(h, w_ref[...])
    k_ref[...] = kv[:, :D_MODEL].astype(BF16)
    v_ref[...] = kv[:, D_MODEL:].astype(BF16)


def _memkv(mem, g, wkv):
    n = mem.shape[0]
    shp = jax.ShapeDtypeStruct((n, D_MODEL), BF16)
    return pl.pallas_call(
        _memkv_body, out_shape=(shp, shp),
        compiler_params=pltpu.CompilerParams(vmem_limit_bytes=VMEM_LIMIT),
        name="mem_kv",
    )(mem, g, wkv)


def _xattn_body(x_ref, *refs, n_mix):
    y_refs, wmix_ref = refs[:n_mix], refs[n_mix]
    g_ref, wq_ref, k_ref, v_ref, wo_ref, o_ref = refs[n_mix + 1:]
    parts = []
    for r in y_refs:
        if len(r.shape) == 3:
            parts.extend(r[j] for j in range(r.shape[0]))
        else:
            parts.append(r[...])
    y = parts[0] if len(parts) == 1 else jnp.concatenate(parts, axis=1)
    x = x_ref[...] + _dot(y, wmix_ref[...])
    h = _rms(x, g_ref[...]).astype(BF16)
    q = (_dot(h, wq_ref[...]) * (XA_HEAD_DIM ** -0.5)).astype(BF16)
    outs = []
    for hh in range(XA_HEADS):
        cols = slice(hh * XA_HEAD_DIM, (hh + 1) * XA_HEAD_DIM)
        s = _dot_nt(q[:, cols], k_ref[:, cols])
        p = jnp.exp(s - jnp.max(s, axis=-1, keepdims=True))
        p = (p / jnp.sum(p, axis=-1, keepdims=True)).astype(BF16)
        outs.append(_dot(p, v_ref[:, cols]).astype(BF16))
    o_ref[...] = x + _dot(jnp.concatenate(outs, axis=1), wo_ref[...])


def _xattn(x, ys, w_mix, g, wq, k, v, wo, layer, *, tm=512):
    s = x.shape[0]
    row = pl.BlockSpec((tm, D_MODEL), lambda i: (i, 0))
    y_specs = []
    for y in ys:
        if y.ndim == 3:
            y_specs.append(pl.BlockSpec((y.shape[0], tm, y.shape[2]), lambda i: (0, i, 0)))
        else:
            y_specs.append(pl.BlockSpec((tm, y.shape[1]), lambda i: (i, 0)))
    return pl.pallas_call(
        functools.partial(_xattn_body, n_mix=len(ys)),
        out_shape=jax.ShapeDtypeStruct((s, D_MODEL), F32),
        grid=(s // tm,),
        in_specs=[row] + y_specs + [_resident(w_mix.shape), _resident((1, D_MODEL)),
                                    _resident(wq.shape, layer), _resident(k.shape), _resident(v.shape),
                                    _resident(wo.shape, layer)],
        out_specs=row,
        compiler_params=_cparams(("arbitrary",)),
        name="mix_out_cross_attn",
    )(x, *ys, w_mix, g, wq, k, v, wo)


def _dnprep_body(x_ref, g_ref, w_ref, wab_ref, wabt_ref, cw_ref,
                 q_ref, k_ref, v_ref, z_ref, ab_ref, abt_ref, ext_ref, *, tm):
    i = pl.program_id(0)
    width = 3 * DN_WIDTH

    @pl.when(i == 0)
    def _():
        ext_ref[0:CONV_PAD] = jnp.zeros((CONV_PAD, width), F32)

    h = _rms(x_ref[...], g_ref[...]).astype(BF16)
    outs = (q_ref, k_ref, v_ref)
    qkv_blocks = width // MXU_DIM
    z_blocks = DN_WIDTH // MXU_DIM
    for b in range(qkv_blocks):
        if b % (qkv_blocks // z_blocks) == 0:
            zc = (b // (qkv_blocks // z_blocks)) * MXU_DIM
            z_ref[:, zc:zc + MXU_DIM] = _dot(h, w_ref[:, width + zc:width + zc + MXU_DIM])
        c0 = b * MXU_DIM
        cols = slice(c0, c0 + MXU_DIM)
        ext_ref[CONV_PAD:CONV_PAD + tm, cols] = _dot(h, w_ref[:, cols])
        cw = cw_ref[:, cols]
        for r0 in range(0, tm, DN_PREP_ROWS):
            act = _silu(_causal_conv(ext_ref[r0:r0 + DN_PREP_ROWS + CONV_PAD, cols], cw, DN_PREP_ROWS))
            for c1 in range(c0, c0 + MXU_DIM, DN_HEAD_DIM):
                part, hh = divmod(c1 // DN_HEAD_DIM, DN_HEADS)
                u = act[:, c1 - c0:c1 - c0 + DN_HEAD_DIM]
                if part < 2:
                    u = u * lax.rsqrt(jnp.sum(u * u, axis=-1, keepdims=True) + L2_EPS)
                if part == 0:
                    u = u * (DN_HEAD_DIM ** -0.5)
                outs[part][hh, r0:r0 + DN_PREP_ROWS, :] = u.astype(BF16)
    ext_ref[0:CONV_PAD] = ext_ref[tm:tm + CONV_PAD]

    ab_ref[...] = _dot(h, wab_ref[...])
    abt = _dot_nt(wabt_ref[...], h)
    for j in range(tm // DN_CHUNK):
        abt_ref[j] = abt[:, j * DN_CHUNK:(j + 1) * DN_CHUNK]


def _dnprep(x, g, w, wab, wabt, cw, *, tm=512):
    s = x.shape[0]
    row = pl.BlockSpec((tm, D_MODEL), lambda i: (i, 0))
    head = pl.BlockSpec((DN_HEADS, tm, DN_HEAD_DIM), lambda i: (0, i, 0))
    head_shape = jax.ShapeDtypeStruct((DN_HEADS, s, DN_HEAD_DIM), BF16)
    nab = wabt.shape[0]
    return pl.pallas_call(
        functools.partial(_dnprep_body, tm=tm),
        out_shape=(head_shape, head_shape, head_shape,
                   jax.ShapeDtypeStruct((s, DN_WIDTH), F32),
                   jax.ShapeDtypeStruct((s, LANES), F32),
                   jax.ShapeDtypeStruct((s // DN_CHUNK, nab, DN_CHUNK), F32)),
        grid=(s // tm,),
        in_specs=[row, _resident((1, D_MODEL)), _resident(w.shape),
                  _resident(wab.shape), _resident(wabt.shape), _resident(cw.shape)],
        out_specs=(head, head, head, row,
                   pl.BlockSpec((tm, LANES), lambda i: (i, 0)),
                   pl.BlockSpec((tm // DN_CHUNK, nab, DN_CHUNK), lambda i: (i, 0, 0))),
        scratch_shapes=[pltpu.VMEM((tm + CONV_PAD, 3 * DN_WIDTH), F32)],
        compiler_params=_cparams(("arbitrary",)),
        name="dn_prep",
    )(x, g, w, wab, wabt, cw)


def _dn_body(q_ref, k_ref, v_ref, z_ref, ab_ref, abt_ref, prow_ref, pcol_ref, on_ref, y_ref, state, *, ts):
    c = DN_CHUNK

    @pl.when(pl.program_id(0) == 0)
    def _():
        state[...] = jnp.zeros((DN_HEADS, DN_HEAD_DIM, DN_HEAD_DIM), F32)

    ri = lax.broadcasted_iota(jnp.int32, (c, c), 0)
    ci = lax.broadcasted_iota(jnp.int32, (c, c), 1)
    lower = ri >= ci
    strict = ri > ci
    tri_l = lower.astype(F32)
    tri_u = (ri <= ci).astype(F32)
    eye = (ri == ci).astype(F32)
    diag_blk = (ri // DN_INV_BLOCK) == (ci // DN_INV_BLOCK)
    off_blks = []
    bsz = DN_INV_BLOCK
    while bsz < c:
        off_blks.append(((ri // (2 * bsz)) == (ci // (2 * bsz))) & ((ri // bsz) != (ci // bsz)))
        bsz *= 2
    hi = lax.Precision.HIGHEST
    a_log_row, dtb_row = prow_ref[0:1, :], prow_ref[1:2, :]
    a_log_col, dtb_col = pcol_ref[:, 0:1], pcol_ref[:, 1:2]
    onorm = on_ref[...]

    heads = range(DN_HEADS)

    def step(it, _):
        rows, gcol, beta, decay, egc, kk, qq, kf = [], [], [], [], [], [], [], []
        for u in range(DN_UNROLL):
            ci_ = it * DN_UNROLL + u
            r = pl.ds(pl.multiple_of(ci_ * c, c), c)
            ab = ab_ref[r, :]
            abt = abt_ref[ci_]
            g_col = -jnp.exp(a_log_row) * _softplus(ab + dtb_row)
            gc_col = jnp.dot(tri_l, g_col, precision=hi, preferred_element_type=F32)
            beta_all = _sigmoid(ab)
            g_row = -jnp.exp(a_log_col) * _softplus(abt + dtb_col)
            gc_row = jnp.dot(g_row, tri_u, precision=hi, preferred_element_type=F32)
            for h in heads:
                rows.append(r)
                gcol.append(gc_col[:, h:h + 1])
                beta.append(beta_all[:, DN_HEADS + h:DN_HEADS + h + 1])
                decay.append(jnp.where(lower, jnp.exp(jnp.minimum(gcol[-1] - gc_row[h:h + 1, :], 0.0)), 0.0))
                egc.append(jnp.exp(gcol[-1]))
                kk.append(k_ref[h, r, :])
                qq.append(q_ref[h, r, :])
                kf.append(kk[-1].astype(F32))
        items = range(DN_UNROLL * DN_HEADS)
        kb = [kf[i] * beta[i] for i in items]
        m = [jnp.where(strict, _dot_nt(kb[i].astype(BF16), kk[i]) * decay[i], 0.0) * -1.0 for i in items]
        qk = [jnp.where(lower, _dot_nt(qq[i], kk[i]) * decay[i], 0.0).astype(BF16) for i in items]
        pw = [jnp.where(diag_blk, m[i], 0.0) for i in items]
        t_inv = [eye + pw[i] for i in items]
        for _ in range(int(math.log2(DN_INV_BLOCK)) - 1):
            pwb = [pw[i].astype(BF16) for i in items]
            pw = [_dot(pwb[i], pwb[i]) for i in items]
            t_inv = [t_inv[i] + _dot(t_inv[i].astype(BF16), pw[i].astype(BF16)) for i in items]
        for off_blk in off_blks:
            tb = [t_inv[i].astype(BF16) for i in items]
            x = [_dot(jnp.where(off_blk, m[i], 0.0).astype(BF16), tb[i]).astype(BF16) for i in items]
            t_inv = [t_inv[i] + _dot(tb[i], x[i]) for i in items]
        rhs = [jnp.concatenate([v_ref[i % DN_HEADS, rows[i], :].astype(F32) * beta[i], kb[i] * egc[i]],
                               axis=1).astype(BF16) for i in items]
        sol = [_dot(t_inv[i].astype(BF16), rhs[i]) for i in items]
        qg = [(qq[i].astype(F32) * egc[i]).astype(BF16) for i in items]
        kd, eg_last = [], []
        for i in items:
            glast = gcol[i][c - 1:c, :]
            kd.append((kf[i] * jnp.exp(glast - gcol[i])).astype(BF16))
            eg_last.append(jnp.exp(glast))
        for u in range(DN_UNROLL):
            idx = [u * DN_HEADS + h for h in heads]
            st = [state[h] for h in heads]
            stb = [st[h].astype(BF16) for h in heads]
            v_new = [sol[i][:, :DN_HEAD_DIM] - _dot(sol[i][:, DN_HEAD_DIM:].astype(BF16), stb[h])
                     for h, i in enumerate(idx)]
            vnb = [v.astype(BF16) for v in v_new]
            for h, i in enumerate(idx):
                state[h] = st[h] * eg_last[i] + _dot_tn(kd[i], vnb[h])
            o = [_dot(qg[i], stb[h]) + _dot(qk[i], vnb[h]) for h, i in enumerate(idx)]
            for h, i in enumerate(idx):
                cols = slice(h * DN_HEAD_DIM, (h + 1) * DN_HEAD_DIM)
                on = o[h] * lax.rsqrt(jnp.mean(o[h] * o[h], axis=-1, keepdims=True) + NORM_EPS) * onorm
                y_ref[rows[i], cols] = (on * _silu(z_ref[rows[i], cols])).astype(BF16)
        return 0

    lax.fori_loop(0, ts // (c * DN_UNROLL), step, 0)


def _deltanet(q, k, v, z, ab, abt, prow, pcol, onorm, *, ts=1024):
    s = z.shape[0]
    head = pl.BlockSpec((DN_HEADS, ts, DN_HEAD_DIM), lambda i: (0, i, 0))
    nab = abt.shape[1]
    return pl.pallas_call(
        functools.partial(_dn_body, ts=ts),
        out_shape=jax.ShapeDtypeStruct((s, DN_WIDTH), BF16),
        grid=(s // ts,),
        in_specs=[head, head, head,
                  pl.BlockSpec((ts, DN_WIDTH), lambda i: (i, 0)),
                  pl.BlockSpec((ts, LANES), lambda i: (i, 0)),
                  pl.BlockSpec((ts // DN_CHUNK, nab, DN_CHUNK), lambda i: (i, 0, 0)),
                  _resident(prow.shape), _resident(pcol.shape), _resident(onorm.shape)],
        out_specs=pl.BlockSpec((ts, DN_WIDTH), lambda i: (i, 0)),
        scratch_shapes=[pltpu.VMEM((DN_HEADS, DN_HEAD_DIM, DN_HEAD_DIM), F32)],
        compiler_params=_cparams(("arbitrary",)),
        name="gated_deltanet",
    )(q, k, v, z, ab, abt, prow, pcol, onorm)


def _row(v):
    return v.reshape(1, -1).astype(F32)


def _block_diag(w):
    n, j, k = w.shape
    eye = jnp.eye(n, dtype=w.dtype)
    return (w[:, :, None, :] * eye[:, None, :, None]).reshape(n * j, n * k)


def kernel(x, mem, ffn1_norm, ffn1_w_in, ffn1_w_out, mix_norm, xa_norm, xa_mem_norm, xa_wq, xa_wkv, xa_wo,
           ffn2_norm, ffn2_w_in, ffn2_w_out, ab_w_in, lru_conv_w, lru_conv_b, lru_w_a, lru_b_a, lru_w_x,
           lru_b_x, lru_lambda, ab_w_out, dn_w_in, dn_conv_w, dn_a_log, dn_dt_bias, dn_o_norm, dn_w_out,
           final_norm):
    batch, seq, _ = x.shape
    depth = ffn1_norm.shape[0]
    ffn1_w_in, ffn1_w_out, ffn2_w_in, ffn2_w_out, xa_wq, xa_wo = (
        w.astype(BF16) for w in (ffn1_w_in, ffn1_w_out, ffn2_w_in, ffn2_w_out, xa_wq, xa_wo))
    outs = []
    for bi in range(batch):
        xs = x[bi]
        ms = mem[bi]
        for layer in range(depth):
            j = layer // 2
            last = layer == depth - 1
            xs = _ffn(xs, _row(ffn1_norm[layer]), ffn1_w_in, ffn1_w_out, _row(final_norm), layer,
                      final_norm=False)
            if layer % 2 == 0:
                q, k, v, xr, gr = _proj0(xs, _row(mix_norm[layer]), ab_w_in[j].astype(BF16))
                attn = _attention(q, k, v)
                y = _lru(xr, gr, lru_conv_w[j].astype(F32), _row(lru_conv_b[j]),
                         _block_diag(lru_w_a[j]).astype(BF16), _row(lru_b_a[j]),
                         _block_diag(lru_w_x[j]).astype(BF16), _row(lru_b_x[j]), _row(lru_lambda[j]))
                mix_ys, mix_w = (attn, y), ab_w_out[j].astype(BF16)
            else:
                w = dn_w_in[j]
                wab = w[:, 4 * DN_WIDTH:]
                n_ab = wab.shape[1]
                wab_pad = jnp.pad(wab, ((0, 0), (0, LANES - n_ab))).astype(BF16)
                q, k, v, z, ab, abt = _dnprep(
                    xs, _row(mix_norm[layer]), w.astype(BF16), wab_pad, wab.T.astype(BF16),
                    dn_conv_w[j].astype(F32))
                pad = jnp.zeros((LANES - DN_HEADS,), F32)
                prow = jnp.stack([jnp.concatenate([dn_a_log[j].astype(F32), pad]),
                                  jnp.concatenate([dn_dt_bias[j].astype(F32), pad])])
                pad2 = jnp.zeros((n_ab - DN_HEADS,), F32)
                pcol = jnp.stack([jnp.concatenate([dn_a_log[j].astype(F32), pad2]),
                                  jnp.concatenate([dn_dt_bias[j].astype(F32), pad2])], axis=1)
                y = _deltanet(q, k, v, z, ab, abt, prow, pcol, _row(dn_o_norm[j]))
                mix_ys, mix_w = (y,), dn_w_out[j].astype(BF16)
            mk, mv = _memkv(ms, _row(xa_mem_norm[layer]), xa_wkv[layer].astype(BF16))
            xs = _xattn(xs, mix_ys, mix_w, _row(xa_norm[layer]), xa_wq, mk, mv, xa_wo, layer)
            xs = _ffn(xs, _row(ffn2_norm[layer]), ffn2_w_in, ffn2_w_out, _row(final_norm), layer,
                      final_norm=last)
        outs.append(xs)
    return jnp.stack(outs, axis=0)
```

```python
import functools
import math

import jax
import jax.numpy as jnp
from jax import lax
from jax.experimental import pallas as pl
from jax.experimental.pallas import tpu as pltpu

F32 = jnp.float32
BF16 = jnp.bfloat16

NORM_EPS = 1e-6
L2_EPS = 1e-6
LANES = 128
SUBLANES = 8

D_MODEL = 1024
D_FF = 2816
MXU_DIM = 256
FF_CHUNKS = (6 * MXU_DIM, 5 * MXU_DIM)
assert sum(FF_CHUNKS) == D_FF
ATTN_HEADS = 8
ATTN_HEAD_DIM = 64
ATTN_WIDTH = 512
ATTN_PAIRS = ATTN_WIDTH // LANES
DILATIONS = (1, 4, 16)
N_BACK = 128
ATTN_BLOCK = 128
ATTN_TILE = 2048
LRU_WIDTH = 512
LRU_C = 8.0
CONV_K = 4
CONV_PAD = SUBLANES
DN_HEADS = 8
DN_HEAD_DIM = 128
DN_WIDTH = 1024
DN_CHUNK = 128
DN_INV_BLOCK = 16
DN_UNROLL = 2
DN_PREP_ROWS = 64
XA_HEADS = 4
XA_HEAD_DIM = 256
NEG_BIG = -1e30
VMEM_LIMIT = 56 * 1024 * 1024


def _cparams(sem):
    return pltpu.CompilerParams(dimension_semantics=sem, vmem_limit_bytes=VMEM_LIMIT)


def _resident(shape, layer=None):
    if layer is None:
        nd = len(shape)
        return pl.BlockSpec(shape, lambda *_: (0,) * nd, pipeline_mode=pl.Buffered(1))
    nd = len(shape) - 1
    return pl.BlockSpec((None,) + tuple(shape[1:]), lambda *_: (layer,) + (0,) * nd,
                        pipeline_mode=pl.Buffered(1))


def _dot(a, b):
    return jnp.dot(a, b, preferred_element_type=F32)


def _dot_nt(a, b):
    return lax.dot_general(a, b, (((1,), (1,)), ((), ())), preferred_element_type=F32)


def _dot_tn(a, b):
    return lax.dot_general(a, b, (((0,), (0,)), ((), ())), preferred_element_type=F32)


def _rms(x, g):
    return x * lax.rsqrt(jnp.mean(x * x, axis=-1, keepdims=True) + NORM_EPS) * g


def _sigmoid(x):
    return 1.0 / (1.0 + jnp.exp(-x))


def _silu(x):
    return x * _sigmoid(x)


def _softplus(x):
    return jnp.maximum(x, 0.0) + jnp.log1p(jnp.exp(-jnp.abs(x)))


def _causal_conv(ext, w, tm):
    y = w[CONV_K - 1:CONV_K, :] * ext[CONV_PAD:CONV_PAD + tm]
    for shift in range(1, CONV_K):
        k = CONV_K - 1 - shift
        y = y + w[k:k + 1, :] * pltpu.roll(ext, shift, 0)[CONV_PAD:CONV_PAD + tm]
    return y


def _gelu_tanh(x):
    c = math.sqrt(2.0 / math.pi)
    return 0.5 * x * (1.0 + jnp.tanh(c * (x + 0.044715 * (x * x * x))))


def _ffn_body(x_ref, g_ref, win_ref, wout_ref, fg_ref, o_ref, *, final_norm):
    x = x_ref[...]
    h = _rms(x, g_ref[...]).astype(BF16)
    acc = jnp.zeros_like(x)
    lo = 0
    for width in FF_CHUNKS:
        gate = _dot(h, win_ref[:, lo:lo + width])
        up = _dot(h, win_ref[:, D_FF + lo:D_FF + lo + width])
        a = (_silu(gate) * up).astype(BF16)
        acc = acc + _dot(a, wout_ref[lo:lo + width, :])
        lo += width
    y = x + 0.5 * acc
    if final_norm:
        y = _rms(y, fg_ref[...])
    o_ref[...] = y


def _ffn(x, g, w_in, w_out, fg, layer, *, final_norm, tm=512):
    s = x.shape[0]
    row = pl.BlockSpec((tm, D_MODEL), lambda i: (i, 0))
    return pl.pallas_call(
        functools.partial(_ffn_body, final_norm=final_norm),
        out_shape=jax.ShapeDtypeStruct((s, D_MODEL), F32),
        grid=(s // tm,),
        in_specs=[row, _resident((1, D_MODEL)), _resident(w_in.shape, layer),
                  _resident(w_out.shape, layer), _resident((1, D_MODEL))],
        out_specs=row,
        compiler_params=_cparams(("arbitrary",)),
        name="ffn_final" if final_norm else "ffn",
    )(x, g, w_in, w_out, fg)


def _proj0_body(x_ref, g_ref, w_ref, *refs, tm):
    nd = len(DILATIONS)
    qkv_refs = (refs[0:nd], refs[nd:2 * nd], refs[2 * nd:3 * nd])
    xr_ref, gr_ref = refs[3 * nd], refs[3 * nd + 1]
    h = _rms(x_ref[...], g_ref[...]).astype(BF16)
    p = _dot(h, w_ref[...])
    xr_ref[...] = p[:, 3 * ATTN_WIDTH:3 * ATTN_WIDTH + LRU_WIDTH]
    gr_ref[...] = p[:, 3 * ATTN_WIDTH + LRU_WIDTH:]
    scale = ATTN_HEAD_DIM ** -0.5
    qkv = jnp.concatenate([p[:, :ATTN_WIDTH] * scale, p[:, ATTN_WIDTH:3 * ATTN_WIDTH]], axis=1).astype(BF16)

    ri = lax.broadcasted_iota(jnp.int32, (MXU_DIM, MXU_DIM), 0)
    ci = lax.broadcasted_iota(jnp.int32, (MXU_DIM, MXU_DIM), 1)
    for gi, d in enumerate(DILATIONS):
        per = MXU_DIM // d
        perm = (ci == (ri % per) * d + ri // per).astype(BF16) if d > 1 else None
        for c in range(tm // MXU_DIM):
            chunk = qkv[c * MXU_DIM:(c + 1) * MXU_DIM]
            if perm is not None:
                chunk = _dot(perm, chunk).astype(BF16)
            for r in range(d):
                piece = chunk[r * per:(r + 1) * per]
                for a in range(3):
                    for j in range(ATTN_PAIRS):
                        col = a * ATTN_WIDTH + j * LANES
                        qkv_refs[a][gi][j, c * per:(c + 1) * per, r * LANES:(r + 1) * LANES] = (
                            piece[:, col:col + LANES])


def _proj0(x, g, w, *, tm=512):
    s = x.shape[0]
    nd = len(DILATIONS)
    row = pl.BlockSpec((tm, D_MODEL), lambda i: (i, 0))
    half = pl.BlockSpec((tm, LRU_WIDTH), lambda i: (i, 0))
    view_shapes = [jax.ShapeDtypeStruct((ATTN_PAIRS, s // d, d * LANES), BF16) for d in DILATIONS]
    view_specs = [pl.BlockSpec((ATTN_PAIRS, tm // d, d * LANES), lambda i: (0, i, 0)) for d in DILATIONS]
    half_shape = jax.ShapeDtypeStruct((s, LRU_WIDTH), F32)
    outs = pl.pallas_call(
        functools.partial(_proj0_body, tm=tm),
        out_shape=tuple(view_shapes * 3) + (half_shape, half_shape),
        grid=(s // tm,),
        in_specs=[row, _resident((1, D_MODEL)), _resident(w.shape)],
        out_specs=tuple(view_specs * 3) + (half, half),
        compiler_params=_cparams(("arbitrary",)),
        name="proj0",
    )(x, g, w)
    return outs[0:nd], outs[nd:2 * nd], outs[2 * nd:3 * nd], outs[3 * nd], outs[3 * nd + 1]


def _attn_body(*refs):
    nd = len(DILATIONS)
    q_refs, k_refs, v_refs = refs[0:nd], refs[nd:2 * nd], refs[2 * nd:3 * nd]
    o_ref = refs[3 * nd]
    khist, vhist = refs[3 * nd + 1:4 * nd + 1], refs[4 * nd + 1:5 * nd + 1]
    og_ref, lg_ref = refs[5 * nd + 1], refs[5 * nd + 2]
    t = pl.program_id(1)

    @pl.when(t == 0)
    def _():
        for h_ref in khist + vhist:
            h_ref[...] = jnp.zeros(h_ref.shape, BF16)

    nk = N_BACK + ATTN_BLOCK
    qi = lax.broadcasted_iota(jnp.int32, (ATTN_BLOCK, nk), 0)
    kj = lax.broadcasted_iota(jnp.int32, (ATTN_BLOCK, nk), 1)
    dist = qi + N_BACK - kj
    band = (dist >= 0) & (dist <= N_BACK)
    bias = jnp.where(band, 0.0, NEG_BIG)
    bias_first = jnp.where(band & (kj >= jnp.where(t == 0, N_BACK, 0)), 0.0, NEG_BIG)
    lane = lax.broadcasted_iota(jnp.int32, (1, LANES), 1)
    head_mask = [(lane < ATTN_HEAD_DIM).astype(BF16), (lane >= ATTN_HEAD_DIM).astype(BF16)]
    lane_lo = lax.broadcasted_iota(jnp.int32, (ATTN_BLOCK, LANES), 1) < ATTN_HEAD_DIM

    for gi, d in enumerate(DILATIONS):
        q_ref, k_ref, v_ref = q_refs[gi], k_refs[gi], v_refs[gi]
        for r in range(d):
            lanes = slice(r * LANES, (r + 1) * LANES)
            for sub in range(ATTN_TILE // d // ATTN_BLOCK):
                lo = ATTN_BLOCK * sub
                q = q_ref[0, lo:lo + ATTN_BLOCK, lanes]
                if sub == 0:
                    kk = jnp.concatenate([khist[gi][:, lanes], k_ref[0, 0:ATTN_BLOCK, lanes]], axis=0)
                    vv = jnp.concatenate([vhist[gi][:, lanes], v_ref[0, 0:ATTN_BLOCK, lanes]], axis=0)
                else:
                    kk = k_ref[0, lo - N_BACK:lo + ATTN_BLOCK, lanes]
                    vv = v_ref[0, lo - N_BACK:lo + ATTN_BLOCK, lanes]
                o_pair = None
                l_pair = None
                for hp in range(2):
                    s = _dot_nt(q * head_mask[hp], kk) + (bias_first if sub == 0 else bias)
                    m = jnp.max(s, axis=-1, keepdims=True)
                    p = jnp.exp(s - m)
                    den = jnp.sum(p, axis=-1, keepdims=True)
                    o = _dot(p.astype(BF16), vv) / den
                    lse = jnp.broadcast_to(m + jnp.log(den), (ATTN_BLOCK, LANES))
                    o_pair = o if hp == 0 else jnp.where(lane_lo, o_pair, o)
                    l_pair = lse if hp == 0 else jnp.where(lane_lo, l_pair, lse)
                tok_rows = pl.ds(r + d * lo, ATTN_BLOCK, stride=d)
                og_ref[gi, tok_rows, :] = o_pair
                lg_ref[gi, tok_rows, :] = l_pair
        rows = ATTN_TILE // d
        khist[gi][...] = k_ref[0, rows - N_BACK:rows, :]
        vhist[gi][...] = v_ref[0, rows - N_BACK:rows, :]

    l0, l1, l2 = lg_ref[0], lg_ref[1], lg_ref[2]
    lm = jnp.maximum(jnp.maximum(l0, l1), l2)
    w0, w1, w2 = jnp.exp(l0 - lm), jnp.exp(l1 - lm), jnp.exp(l2 - lm)
    out = (w0 * og_ref[0] + w1 * og_ref[1] + w2 * og_ref[2]) / (w0 + w1 + w2)
    o_ref[0] = out.astype(BF16)


def _attention(q_views, k_views, v_views):
    s = q_views[0].shape[1]
    views = list(q_views) + list(k_views) + list(v_views)
    specs = [pl.BlockSpec((1, ATTN_TILE // d, d * LANES), lambda p, t: (p, t, 0)) for d in DILATIONS] * 3
    hist = [pltpu.VMEM((N_BACK, d * LANES), BF16) for d in DILATIONS]
    return pl.pallas_call(
        _attn_body,
        out_shape=jax.ShapeDtypeStruct((ATTN_PAIRS, s, LANES), BF16),
        grid=(ATTN_PAIRS, s // ATTN_TILE),
        in_specs=specs,
        out_specs=pl.BlockSpec((1, ATTN_TILE, LANES), lambda p, t: (p, t, 0)),
        scratch_shapes=hist + hist + [pltpu.VMEM((len(DILATIONS), ATTN_TILE, LANES), F32),
                                      pltpu.VMEM((len(DILATIONS), ATTN_TILE, LANES), F32)],
        compiler_params=_cparams(("arbitrary", "arbitrary")),
        name="dilated_attn",
    )(*views)


def _scan_step(a, b, k, pos):
    ok = pos >= k
    a_sh = pltpu.roll(a, k, 0)
    b_sh = pltpu.roll(b, k, 0)
    return jnp.where(ok, a * a_sh, a), jnp.where(ok, a * b_sh + b, b)


def _lru_body(xr_ref, gr_ref, cw_ref, cb_ref, wa_ref, ba_ref, wx_ref, bx_ref, lam_ref, y_ref,
              ext_ref, carry_ref, a_s, b_s, cin_s, *, tm):
    i = pl.program_id(0)

    @pl.when(i == 0)
    def _():
        ext_ref[0:CONV_PAD] = jnp.zeros((CONV_PAD, LRU_WIDTH), F32)
        carry_ref[...] = jnp.zeros((SUBLANES, LRU_WIDTH), F32)

    @pl.when(i > 0)
    def _():
        ext_ref[0:CONV_PAD] = ext_ref[tm:tm + CONV_PAD]

    ext_ref[CONV_PAD:CONV_PAD + tm] = xr_ref[...]
    xc = _causal_conv(ext_ref[...], cw_ref[...], tm) + cb_ref[...]
    xcb = xc.astype(BF16)
    r = _sigmoid(_dot(xcb, wa_ref[...]) + ba_ref[...])
    ig = _sigmoid(_dot(xcb, wx_ref[...]) + bx_ref[...])
    log_a = -LRU_C * r * _softplus(-lam_ref[...])
    a = jnp.exp(log_a)
    b = jnp.sqrt(-jnp.tanh(log_a) * (a * a + 1.0)) * ig * xc

    pos = lax.broadcasted_iota(jnp.int32, (tm, LRU_WIDTH), 0) & (SUBLANES - 1)
    for k in (1, 2, 4):
        a, b = _scan_step(a, b, k, pos)
    lane_groups = [slice(g * LANES, (g + 1) * LANES) for g in range(LRU_WIDTH // LANES)]
    for g, cols in enumerate(lane_groups):
        a_s[g] = a[:, cols]
        b_s[g] = b[:, cols]
    ng = tm // SUBLANES
    ends = pl.ds(SUBLANES - 1, ng, stride=SUBLANES)
    ae = jnp.concatenate([a_s[g, ends, :] for g in range(len(lane_groups))], axis=1)
    be = jnp.concatenate([b_s[g, ends, :] for g in range(len(lane_groups))], axis=1)
    gpos = lax.broadcasted_iota(jnp.int32, (ng, LRU_WIDTH), 0)
    k = 1
    while k < ng:
        ae, be = _scan_step(ae, be, k, gpos)
        k *= 2
    carry = carry_ref[0:1, :]
    h_end = ae * carry + be
    h_in = jnp.where(gpos >= 1, pltpu.roll(h_end, 1, 0), carry)
    for g, cols in enumerate(lane_groups):
        for j in range(SUBLANES):
            cin_s[g, pl.ds(j, ng, stride=SUBLANES), :] = h_in[:, cols]
    carry_ref[...] = jnp.broadcast_to(h_end[ng - 1:ng, :], (SUBLANES, LRU_WIDTH))
    h = jnp.concatenate([a_s[g] * cin_s[g] + b_s[g] for g in range(len(lane_groups))], axis=1)
    y_ref[...] = (h * _gelu_tanh(gr_ref[...])).astype(BF16)


def _lru(xr, gr, cw, cb, wa, ba, wx, bx, lam, *, tm=1024):
    s = xr.shape[0]
    row = pl.BlockSpec((tm, LRU_WIDTH), lambda i: (i, 0))
    vec = _resident((1, LRU_WIDTH))
    sq = _resident((LRU_WIDTH, LRU_WIDTH))
    return pl.pallas_call(
        functools.partial(_lru_body, tm=tm),
        out_shape=jax.ShapeDtypeStruct((s, LRU_WIDTH), BF16),
        grid=(s // tm,),
        in_specs=[row, row, _resident((CONV_K, LRU_WIDTH)), vec, sq, vec, sq, vec, vec],
        out_specs=row,
        scratch_shapes=[pltpu.VMEM((tm + CONV_PAD, LRU_WIDTH), F32),
                        pltpu.VMEM((SUBLANES, LRU_WIDTH), F32),
                        pltpu.VMEM((LRU_WIDTH // LANES, tm, LANES), F32),
                        pltpu.VMEM((LRU_WIDTH // LANES, tm, LANES), F32),
                        pltpu.VMEM((LRU_WIDTH // LANES, tm, LANES), F32)],
        compiler_params=_cparams(("arbitrary",)),
        name="rglru",
    )(xr, gr, cw, cb, wa, ba, wx, bx, lam)


def _memkv_body(m_ref, g_ref, w_ref, k_ref, v_ref):
    h = _rms(m_ref[...], g_ref[...]).astype(BF16)
    kv = _dot(h, w_ref[...])
    k_ref[...] = kv[:, :D_MODEL].astype(BF16)
    v_ref[...] = kv[:, D_MODEL:].astype(BF16)


def _memkv(mem, g, wkv):
    n = mem.shape[0]
    shp = jax.ShapeDtypeStruct((n, D_MODEL), BF16)
    return pl.pallas_call(
        _memkv_body, out_shape=(shp, shp),
        compiler_params=pltpu.CompilerParams(vmem_limit_bytes=VMEM_LIMIT),
        name="mem_kv",
    )(mem, g, wkv)


def _xattn_body(x_ref, *refs, n_mix):
    y_refs, wmix_ref = refs[:n_mix], refs[n_mix]
    g_ref, wq_ref, k_ref, v_ref, wo_ref, o_ref = refs[n_mix + 1:]
    parts = []
    for r in y_refs:
        if len(r.shape) == 3:
            parts.extend(r[j] for j in range(r.shape[0]))
        else:
            parts.append(r[...])
    y = parts[0] if len(parts) == 1 else jnp.concatenate(parts, axis=1)
    x = x_ref[...] + _dot(y, wmix_ref[...])
    h = _rms(x, g_ref[...]).astype(BF16)
    q = (_dot(h, wq_ref[...]) * (XA_HEAD_DIM ** -0.5)).astype(BF16)
    outs = []
    for hh in range(XA_HEADS):
        cols = slice(hh * XA_HEAD_DIM, (hh + 1) * XA_HEAD_DIM)
        s = _dot_nt(q[:, cols], k_ref[:, cols])
        p = jnp.exp(s - jnp.max(s, axis=-1, keepdims=True))
        p = (p / jnp.sum(p, axis=-1, keepdims=True)).astype(BF16)
        outs.append(_dot(p, v_ref[:, cols]).astype(BF16))
    o_ref[...] = x + _dot(jnp.concatenate(outs, axis=1), wo_ref[...])


def _xattn(x, ys, w_mix, g, wq, k, v, wo, layer, *, tm=512):
    s = x.shape[0]
    row = pl.BlockSpec((tm, D_MODEL), lambda i: (i, 0))
    y_specs = []
    for y in ys:
        if y.ndim == 3:
            y_specs.append(pl.BlockSpec((y.shape[0], tm, y.shape[2]), lambda i: (0, i, 0)))
        else:
            y_specs.append(pl.BlockSpec((tm, y.shape[1]), lambda i: (i, 0)))
    return pl.pallas_call(
        functools.partial(_xattn_body, n_mix=len(ys)),
        out_shape=jax.ShapeDtypeStruct((s, D_MODEL), F32),
        grid=(s // tm,),
        in_specs=[row] + y_specs + [_resident(w_mix.shape), _resident((1, D_MODEL)),
                                    _resident(wq.shape, layer), _resident(k.shape), _resident(v.shape),
                                    _resident(wo.shape, layer)],
        out_specs=row,
        compiler_params=_cparams(("arbitrary",)),
        name="mix_out_cross_attn",
    )(x, *ys, w_mix, g, wq, k, v, wo)


def _dnprep_body(x_ref, g_ref, w_ref, wab_ref, wabt_ref, cw_ref,
                 q_ref, k_ref, v_ref, z_ref, ab_ref, abt_ref, ext_ref, *, tm):
    i = pl.program_id(0)
    width = 3 * DN_WIDTH

    @pl.when(i == 0)
    def _():
        ext_ref[0:CONV_PAD] = jnp.zeros((CONV_PAD, width), F32)

    h = _rms(x_ref[...], g_ref[...]).astype(BF16)
    outs = (q_ref, k_ref, v_ref)
    qkv_blocks = width // MXU_DIM
    z_blocks = DN_WIDTH // MXU_DIM
    for b in range(qkv_blocks):
        if b % (qkv_blocks // z_blocks) == 0:
            zc = (b // (qkv_blocks // z_blocks)) * MXU_DIM
            z_ref[:, zc:zc + MXU_DIM] = _dot(h, w_ref[:, width + zc:width + zc + MXU_DIM])
        c0 = b * MXU_DIM
        cols = slice(c0, c0 + MXU_DIM)
        ext_ref[CONV_PAD:CONV_PAD + tm, cols] = _dot(h, w_ref[:, cols])
        cw = cw_ref[:, cols]
        for r0 in range(0, tm, DN_PREP_ROWS):
            act = _silu(_causal_conv(ext_ref[r0:r0 + DN_PREP_ROWS + CONV_PAD, cols], cw, DN_PREP_ROWS))
            for c1 in range(c0, c0 + MXU_DIM, DN_HEAD_DIM):
                part, hh = divmod(c1 // DN_HEAD_DIM, DN_HEADS)
                u = act[:, c1 - c0:c1 - c0 + DN_HEAD_DIM]
                if part < 2:
                    u = u * lax.rsqrt(jnp.sum(u * u, axis=-1, keepdims=True) + L2_EPS)
                if part == 0:
                    u = u * (DN_HEAD_DIM ** -0.5)
                outs[part][hh, r0:r0 + DN_PREP_ROWS, :] = u.astype(BF16)
    ext_ref[0:CONV_PAD] = ext_ref[tm:tm + CONV_PAD]

    ab_ref[...] = _dot(h, wab_ref[...])
    abt = _dot_nt(wabt_ref[...], h)
    for j in range(tm // DN_CHUNK):
        abt_ref[j] = abt[:, j * DN_CHUNK:(j + 1) * DN_CHUNK]


def _dnprep(x, g, w, wab, wabt, cw, *, tm=512):
    s = x.shape[0]
    row = pl.BlockSpec((tm, D_MODEL), lambda i: (i, 0))
    head = pl.BlockSpec((DN_HEADS, tm, DN_HEAD_DIM), lambda i: (0, i, 0))
    head_shape = jax.ShapeDtypeStruct((DN_HEADS, s, DN_HEAD_DIM), BF16)
    nab = wabt.shape[0]
    return pl.pallas_call(
        functools.partial(_dnprep_body, tm=tm),
        out_shape=(head_shape, head_shape, head_shape,
                   jax.ShapeDtypeStruct((s, DN_WIDTH), F32),
                   jax.ShapeDtypeStruct((s, LANES), F32),
                   jax.ShapeDtypeStruct((s // DN_CHUNK, nab, DN_CHUNK), F32)),
        grid=(s // tm,),
        in_specs=[row, _resident((1, D_MODEL)), _resident(w.shape),
                  _resident(wab.shape), _resident(wabt.shape), _resident(cw.shape)],
        out_specs=(head, head, head, row,
                   pl.BlockSpec((tm, LANES), lambda i: (i, 0)),
                   pl.BlockSpec((tm // DN_CHUNK, nab, DN_CHUNK), lambda i: (i, 0, 0))),
        scratch_shapes=[pltpu.VMEM((tm + CONV_PAD, 3 * DN_WIDTH), F32)],
        compiler_params=_cparams(("arbitrary",)),
        name="dn_prep",
    )(x, g, w, wab, wabt, cw)


def _dn_body(q_ref, k_ref, v_ref, z_ref, ab_ref, abt_ref, prow_ref, pcol_ref, on_ref, y_ref, state, *, ts):
    c = DN_CHUNK

    @pl.when(pl.program_id(0) == 0)
    def _():
        state[...] = jnp.zeros((DN_HEADS, DN_HEAD_DIM, DN_HEAD_DIM), F32)

    ri = lax.broadcasted_iota(jnp.int32, (c, c), 0)
    ci = lax.broadcasted_iota(jnp.int32, (c, c), 1)
    lower = ri >= ci
    strict = ri > ci
    tri_l = lower.astype(F32)
    tri_u = (ri <= ci).astype(F32)
    eye = (ri == ci).astype(F32)
    diag_blk = (ri // DN_INV_BLOCK) == (ci // DN_INV_BLOCK)
    off_blks = []
    bsz = DN_INV_BLOCK
    while bsz < c:
        off_blks.append(((ri // (2 * bsz)) == (ci // (2 * bsz))) & ((ri // bsz) != (ci // bsz)))
        bsz *= 2
    hi = lax.Precision.HIGHEST
    a_log_row, dtb_row = prow_ref[0:1, :], prow_ref[1:2, :]
    a_log_col, dtb_col = pcol_ref[:, 0:1], pcol_ref[:, 1:2]
    onorm = on_ref[...]

    heads = range(DN_HEADS)

    def step(it, _):
        rows, gcol, beta, decay, egc, kk, qq, kf = [], [], [], [], [], [], [], []
        for u in range(DN_UNROLL):
            ci_ = it * DN_UNROLL + u
            r = pl.ds(pl.multiple_of(ci_ * c, c), c)
            ab = ab_ref[r, :]
            abt = abt_ref[ci_]
            g_col = -jnp.exp(a_log_row) * _softplus(ab + dtb_row)
            gc_col = jnp.dot(tri_l, g_col, precision=hi, preferred_element_type=F32)
            beta_all = _sigmoid(ab)
            g_row = -jnp.exp(a_log_col) * _softplus(abt + dtb_col)
            gc_row = jnp.dot(g_row, tri_u, precision=hi, preferred_element_type=F32)
            for h in heads:
                rows.append(r)
                gcol.append(gc_col[:, h:h + 1])
                beta.append(beta_all[:, DN_HEADS + h:DN_HEADS + h + 1])
                decay.append(jnp.where(lower, jnp.exp(jnp.minimum(gcol[-1] - gc_row[h:h + 1, :], 0.0)), 0.0))
                egc.append(jnp.exp(gcol[-1]))
                kk.append(k_ref[h, r, :])
                qq.append(q_ref[h, r, :])
                kf.append(kk[-1].astype(F32))
        items = range(DN_UNROLL * DN_HEADS)
        kb = [kf[i] * beta[i] for i in items]
        m = [jnp.where(strict, _dot_nt(kb[i].astype(BF16), kk[i]) * decay[i], 0.0) * -1.0 for i in items]
        qk = [jnp.where(lower, _dot_nt(qq[i], kk[i]) * decay[i], 0.0).astype(BF16) for i in items]
        pw = [jnp.where(diag_blk, m[i], 0.0) for i in items]
        t_inv = [eye + pw[i] for i in items]
        for _ in range(int(math.log2(DN_INV_BLOCK)) - 1):
            pwb = [pw[i].astype(BF16) for i in items]
            pw = [_dot(pwb[i], pwb[i]) for i in items]
            t_inv = [t_inv[i] + _dot(t_inv[i].astype(BF16), pw[i].astype(BF16)) for i in items]
        for off_blk in off_blks:
            tb = [t_inv[i].astype(BF16) for i in items]
            x = [_dot(jnp.where(off_blk, m[i], 0.0).astype(BF16), tb[i]).astype(BF16) for i in items]
            t_inv = [t_inv[i] + _dot(tb[i], x[i]) for i in items]
        rhs = [jnp.concatenate([v_ref[i % DN_HEADS, rows[i], :].astype(F32) * beta[i], kb[i] * egc[i]],
                               axis=1).astype(BF16) for i in items]
        sol = [_dot(t_inv[i].astype(BF16), rhs[i]) for i in items]
        qg = [(qq[i].astype(F32) * egc[i]).astype(BF16) for i in items]
        kd, eg_last = [], []
        for i in items:
            glast = gcol[i][c - 1:c, :]
            kd.append((kf[i] * jnp.exp(glast - gcol[i])).astype(BF16))
            eg_last.append(jnp.exp(glast))
        for u in range(DN_UNROLL):
            idx = [u * DN_HEADS + h for h in heads]
            st = [state[h] for h in heads]
            stb = [st[h].astype(BF16) for h in heads]
            v_new = [sol[i][:, :DN_HEAD_DIM] - _dot(sol[i][:, DN_HEAD_DIM:].astype(BF16), stb[h])
                     for h, i in enumerate(idx)]
            vnb = [v.astype(BF16) for v in v_new]
            for h, i in enumerate(idx):
                state[h] = st[h] * eg_last[i] + _dot_tn(kd[i], vnb[h])
            o = [_dot(qg[i], stb[h]) + _dot(qk[i], vnb[h]) for h, i in enumerate(idx)]
            for h, i in enumerate(idx):
                cols = slice(h * DN_HEAD_DIM, (h + 1) * DN_HEAD_DIM)
                on = o[h] * lax.rsqrt(jnp.mean(o[h] * o[h], axis=-1, keepdims=True) + NORM_EPS) * onorm
                y_ref[rows[i], cols] = (on * _silu(z_ref[rows[i], cols])).astype(BF16)
        return 0

    lax.fori_loop(0, ts // (c * DN_UNROLL), step, 0)


def _deltanet(q, k, v, z, ab, abt, prow, pcol, onorm, *, ts=1024):
    s = z.shape[0]
    head = pl.BlockSpec((DN_HEADS, ts, DN_HEAD_DIM), lambda i: (0, i, 0))
    nab = abt.shape[1]
    return pl.pallas_call(
        functools.partial(_dn_body, ts=ts),
        out_shape=jax.ShapeDtypeStruct((s, DN_WIDTH), BF16),
        grid=(s // ts,),
        in_specs=[head, head, head,
                  pl.BlockSpec((ts, DN_WIDTH), lambda i: (i, 0)),
                  pl.BlockSpec((ts, LANES), lambda i: (i, 0)),
                  pl.BlockSpec((ts // DN_CHUNK, nab, DN_CHUNK), lambda i: (i, 0, 0)),
                  _resident(prow.shape), _resident(pcol.shape), _resident(onorm.shape)],
        out_specs=pl.BlockSpec((ts, DN_WIDTH), lambda i: (i, 0)),
        scratch_shapes=[pltpu.VMEM((DN_HEADS, DN_HEAD_DIM, DN_HEAD_DIM), F32)],
        compiler_params=_cparams(("arbitrary",)),
        name="gated_deltanet",
    )(q, k, v, z, ab, abt, prow, pcol, onorm)


def _row(v):
    return v.reshape(1, -1).astype(F32)


def _block_diag(w):
    n, j, k = w.shape
    eye = jnp.eye(n, dtype=w.dtype)
    return (w[:, :, None, :] * eye[:, None, :, None]).reshape(n * j, n * k)


def kernel(x, mem, ffn1_norm, ffn1_w_in, ffn1_w_out, mix_norm, xa_norm, xa_mem_norm, xa_wq, xa_wkv, xa_wo,
           ffn2_norm, ffn2_w_in, ffn2_w_out, ab_w_in, lru_conv_w, lru_conv_b, lru_w_a, lru_b_a, lru_w_x,
           lru_b_x, lru_lambda, ab_w_out, dn_w_in, dn_conv_w, dn_a_log, dn_dt_bias, dn_o_norm, dn_w_out,
           final_norm):
    batch, seq, _ = x.shape
    depth = ffn1_norm.shape[0]
    ffn1_w_in, ffn1_w_out, ffn2_w_in, ffn2_w_out, xa_wq, xa_wo = (
        w.astype(BF16) for w in (ffn1_w_in, ffn1_w_out, ffn2_w_in, ffn2_w_out, xa_wq, xa_wo))
    outs = []
    for bi in range(batch):
        xs = x[bi]
        ms = mem[bi]
        for layer in range(depth):
            j = layer // 2
            last = layer == depth - 1
            xs = _ffn(xs, _row(ffn1_norm[layer]), ffn1_w_in, ffn1_w_out, _row(final_norm), layer,
                      final_norm=False)
            if layer % 2 == 0:
                q, k, v, xr, gr = _proj0(xs, _row(mix_norm[layer]), ab_w_in[j].astype(BF16))
                attn = _attention(q, k, v)
                y = _lru(xr, gr, lru_conv_w[j].astype(F32), _row(lru_conv_b[j]),
                         _block_diag(lru_w_a[j]).astype(BF16), _row(lru_b_a[j]),
                         _block_diag(lru_w_x[j]).astype(BF16), _row(lru_b_x[j]), _row(lru_lambda[j]))
                mix_ys, mix_w = (attn, y), ab_w_out[j].astype(BF16)
            else:
                w = dn_w_in[j]
                wab = w[:, 4 * DN_WIDTH:]
                n_ab = wab.shape[1]
                wab_pad = jnp.pad(wab, ((0, 0), (0, LANES - n_ab))).astype(BF16)
                q, k, v, z, ab, abt = _dnprep(
                    xs, _row(mix_norm[layer]), w.astype(BF16), wab_pad, wab.T.astype(BF16),
                    dn_conv_w[j].astype(F32))
                pad = jnp.zeros((LANES - DN_HEADS,), F32)
                prow = jnp.stack([jnp.concatenate([dn_a_log[j].astype(F32), pad]),
                                  jnp.concatenate([dn_dt_bias[j].astype(F32), pad])])
                pad2 = jnp.zeros((n_ab - DN_HEADS,), F32)
                pcol = jnp.stack([jnp.concatenate([dn_a_log[j].astype(F32), pad2]),
                                  jnp.concatenate([dn_dt_bias[j].astype(F32), pad2])], axis=1)
                y = _deltanet(q, k, v, z, ab, abt, prow, pcol, _row(dn_o_norm[j]))
                mix_ys, mix_w = (y,), dn_w_out[j].astype(BF16)
            mk, mv = _memkv(ms, _row(xa_mem_norm[layer]), xa_wkv[layer].astype(BF16))
            xs = _xattn(xs, mix_ys, mix_w, _row(xa_norm[layer]), xa_wq, mk, mv, xa_wo, layer)
            xs = _ffn(xs, _row(ffn2_norm[layer]), ffn2_w_in, ffn2_w_out, _row(final_norm), layer,
                      final_norm=last)
        outs.append(xs)
    return jnp.stack(outs, axis=0)
```

```python
import functools
import math

import jax
import jax.numpy as jnp
from jax import lax
from jax.experimental import pallas as pl
from jax.experimental.pallas import tpu as pltpu

F32 = jnp.float32
BF16 = jnp.bfloat16

NORM_EPS = 1e-6
L2_EPS = 1e-6
LANES = 128
SUBLANES = 8

D_MODEL = 1024
D_FF = 2816
MXU_DIM = 256
FF_CHUNKS = (6 * MXU_DIM, 5 * MXU_DIM)
assert sum(FF_CHUNKS) == D_FF
FF_STAGE_CHUNKS = 8
ATTN_HEADS = 8
ATTN_HEAD_DIM = 64
ATTN_WIDTH = 512
ATTN_PAIRS = ATTN_WIDTH // LANES
DILATIONS = (1, 4, 16)
N_BACK = 128
ATTN_BLOCK = 128
ATTN_TILE = 2048
LRU_WIDTH = 512
LRU_C = 8.0
CONV_K = 4
CONV_PAD = SUBLANES
DN_HEADS = 8
DN_HEAD_DIM = 128
DN_WIDTH = 1024
DN_CHUNK = 128
DN_INV_BLOCK = 16
DN_UNROLL = 2
DN_PREP_ROWS = 64
XA_HEADS = 4
XA_HEAD_DIM = 256
NEG_BIG = -1e30
VMEM_LIMIT = 56 * 1024 * 1024


def _cparams(sem):
    return pltpu.CompilerParams(dimension_semantics=sem, vmem_limit_bytes=VMEM_LIMIT)


def _resident(shape, layer=None):
    if layer is None:
        nd = len(shape)
        return pl.BlockSpec(shape, lambda *_: (0,) * nd, pipeline_mode=pl.Buffered(1))
    nd = len(shape) - 1
    return pl.BlockSpec((None,) + tuple(shape[1:]), lambda *_: (layer,) + (0,) * nd,
                        pipeline_mode=pl.Buffered(1))


def _dot(a, b):
    return jnp.dot(a, b, preferred_element_type=F32)


def _dot_nt(a, b):
    return lax.dot_general(a, b, (((1,), (1,)), ((), ())), preferred_element_type=F32)


def _dot_tn(a, b):
    return lax.dot_general(a, b, (((0,), (0,)), ((), ())), preferred_element_type=F32)


def _rms(x, g):
    return x * lax.rsqrt(jnp.mean(x * x, axis=-1, keepdims=True) + NORM_EPS) * g


def _sigmoid(x):
    return 1.0 / (1.0 + jnp.exp(-x))


def _silu(x):
    return x * _sigmoid(x)


def _softplus(x):
    return jnp.maximum(x, 0.0) + jnp.log1p(jnp.exp(-jnp.abs(x)))


def _causal_conv(ext, w, tm):
    y = w[CONV_K - 1:CONV_K, :] * ext[CONV_PAD:CONV_PAD + tm]
    for shift in range(1, CONV_K):
        k = CONV_K - 1 - shift
        y = y + w[k:k + 1, :] * pltpu.roll(ext, shift, 0)[CONV_PAD:CONV_PAD + tm]
    return y


def _gelu_tanh(x):
    c = math.sqrt(2.0 / math.pi)
    return 0.5 * x * (1.0 + jnp.tanh(c * (x + 0.044715 * (x * x * x))))


def _stage_weight(src_hbm, layer, dst, stage, sem):
    rows = stage.shape[1]
    n = dst.shape[0] // rows

    def copy(c):
        return pltpu.make_async_copy(src_hbm.at[layer, pl.ds(c * rows, rows), :], stage.at[c % 2], sem.at[c % 2])

    copy(0).start()
    for c in range(n):
        if c + 1 < n:
            copy(c + 1).start()
        copy(c).wait()
        dst[c * rows:(c + 1) * rows, :] = stage[c % 2].astype(BF16)


def _staged_weight_scratch(shape):
    rows, cols = shape
    return [pltpu.VMEM((rows, cols), BF16), pltpu.VMEM((2, rows // FF_STAGE_CHUNKS, cols), F32),
            pltpu.SemaphoreType.DMA((2,))]


def _ffn_body(x_ref, g_ref, win_hbm, wout_hbm, fg_ref, o_ref, win_ref, sin, sem_in, wout_ref, sout, sem_out,
              *, layer, final_norm):
    @pl.when(pl.program_id(0) == 0)
    def _():
        _stage_weight(win_hbm, layer, win_ref, sin, sem_in)
        _stage_weight(wout_hbm, layer, wout_ref, sout, sem_out)

    x = x_ref[...]
    h = _rms(x, g_ref[...]).astype(BF16)
    acc = jnp.zeros_like(x)
    lo = 0
    for width in FF_CHUNKS:
        gate = _dot(h, win_ref[:, lo:lo + width])
        up = _dot(h, win_ref[:, D_FF + lo:D_FF + lo + width])
        a = (_silu(gate) * up).astype(BF16)
        acc = acc + _dot(a, wout_ref[lo:lo + width, :])
        lo += width
    y = x + 0.5 * acc
    if final_norm:
        y = _rms(y, fg_ref[...])
    o_ref[...] = y


def _ffn(x, g, w_in, w_out, fg, layer, *, final_norm, tm=512):
    s = x.shape[0]
    row = pl.BlockSpec((tm, D_MODEL), lambda i: (i, 0))
    hbm = pl.BlockSpec(memory_space=pl.ANY)
    return pl.pallas_call(
        functools.partial(_ffn_body, layer=layer, final_norm=final_norm),
        out_shape=jax.ShapeDtypeStruct((s, D_MODEL), F32),
        grid=(s // tm,),
        in_specs=[row, _resident((1, D_MODEL)), hbm, hbm, _resident((1, D_MODEL))],
        out_specs=row,
        scratch_shapes=_staged_weight_scratch(w_in.shape[1:]) + _staged_weight_scratch(w_out.shape[1:]),
        compiler_params=_cparams(("arbitrary",)),
        name="ffn_final" if final_norm else "ffn",
    )(x, g, w_in, w_out, fg)


def _proj0_body(x_ref, g_ref, w_hbm, *refs, tm, layer):
    nd = len(DILATIONS)
    qkv_refs = (refs[0:nd], refs[nd:2 * nd], refs[2 * nd:3 * nd])
    xr_ref, gr_ref = refs[3 * nd], refs[3 * nd + 1]
    w_ref, stage, sem = refs[3 * nd + 2:]

    @pl.when(pl.program_id(0) == 0)
    def _():
        _stage_weight(w_hbm, layer, w_ref, stage, sem)

    h = _rms(x_ref[...], g_ref[...]).astype(BF16)
    p = _dot(h, w_ref[...])
    xr_ref[...] = p[:, 3 * ATTN_WIDTH:3 * ATTN_WIDTH + LRU_WIDTH]
    gr_ref[...] = p[:, 3 * ATTN_WIDTH + LRU_WIDTH:]
    scale = ATTN_HEAD_DIM ** -0.5
    qkv = jnp.concatenate([p[:, :ATTN_WIDTH] * scale, p[:, ATTN_WIDTH:3 * ATTN_WIDTH]], axis=1).astype(BF16)

    ri = lax.broadcasted_iota(jnp.int32, (MXU_DIM, MXU_DIM), 0)
    ci = lax.broadcasted_iota(jnp.int32, (MXU_DIM, MXU_DIM), 1)
    for gi, d in enumerate(DILATIONS):
        per = MXU_DIM // d
        perm = (ci == (ri % per) * d + ri // per).astype(BF16) if d > 1 else None
        for c in range(tm // MXU_DIM):
            chunk = qkv[c * MXU_DIM:(c + 1) * MXU_DIM]
            if perm is not None:
                chunk = _dot(perm, chunk).astype(BF16)
            for r in range(d):
                piece = chunk[r * per:(r + 1) * per]
                for a in range(3):
                    for j in range(ATTN_PAIRS):
                        col = a * ATTN_WIDTH + j * LANES
                        qkv_refs[a][gi][j, c * per:(c + 1) * per, r * LANES:(r + 1) * LANES] = (
                            piece[:, col:col + LANES])


def _proj0(x, g, w, layer, *, tm=512):
    s = x.shape[0]
    nd = len(DILATIONS)
    row = pl.BlockSpec((tm, D_MODEL), lambda i: (i, 0))
    half = pl.BlockSpec((tm, LRU_WIDTH), lambda i: (i, 0))
    view_shapes = [jax.ShapeDtypeStruct((ATTN_PAIRS, s // d, d * LANES), BF16) for d in DILATIONS]
    view_specs = [pl.BlockSpec((ATTN_PAIRS, tm // d, d * LANES), lambda i: (0, i, 0)) for d in DILATIONS]
    half_shape = jax.ShapeDtypeStruct((s, LRU_WIDTH), F32)
    outs = pl.pallas_call(
        functools.partial(_proj0_body, tm=tm, layer=layer),
        out_shape=tuple(view_shapes * 3) + (half_shape, half_shape),
        grid=(s // tm,),
        in_specs=[row, _resident((1, D_MODEL)), pl.BlockSpec(memory_space=pl.ANY)],
        out_specs=tuple(view_specs * 3) + (half, half),
        scratch_shapes=_staged_weight_scratch(w.shape[1:]),
        compiler_params=_cparams(("arbitrary",)),
        name="proj0",
    )(x, g, w)
    return outs[0:nd], outs[nd:2 * nd], outs[2 * nd:3 * nd], outs[3 * nd], outs[3 * nd + 1]


def _attn_body(*refs):
    nd = len(DILATIONS)
    q_refs, k_refs, v_refs = refs[0:nd], refs[nd:2 * nd], refs[2 * nd:3 * nd]
    o_ref = refs[3 * nd]
    khist, vhist = refs[3 * nd + 1:4 * nd + 1], refs[4 * nd + 1:5 * nd + 1]
    og_ref, lg_ref = refs[5 * nd + 1], refs[5 * nd + 2]
    t = pl.program_id(1)

    @pl.when(t == 0)
    def _():
        for h_ref in khist + vhist:
            h_ref[...] = jnp.zeros(h_ref.shape, BF16)

    nk = N_BACK + ATTN_BLOCK
    qi = lax.broadcasted_iota(jnp.int32, (ATTN_BLOCK, nk), 0)
    kj = lax.broadcasted_iota(jnp.int32, (ATTN_BLOCK, nk), 1)
    dist = qi + N_BACK - kj
    band = (dist >= 0) & (dist <= N_BACK)
    bias = jnp.where(band, 0.0, NEG_BIG)
    bias_first = jnp.where(band & (kj >= jnp.where(t == 0, N_BACK, 0)), 0.0, NEG_BIG)
    lane = lax.broadcasted_iota(jnp.int32, (1, LANES), 1)
    head_mask = [(lane < ATTN_HEAD_DIM).astype(BF16), (lane >= ATTN_HEAD_DIM).astype(BF16)]
    lane_lo = lax.broadcasted_iota(jnp.int32, (ATTN_BLOCK, LANES), 1) < ATTN_HEAD_DIM

    for gi, d in enumerate(DILATIONS):
        q_ref, k_ref, v_ref = q_refs[gi], k_refs[gi], v_refs[gi]
        for r in range(d):
            lanes = slice(r * LANES, (r + 1) * LANES)
            for sub in range(ATTN_TILE // d // ATTN_BLOCK):
                lo = ATTN_BLOCK * sub
                q = q_ref[0, lo:lo + ATTN_BLOCK, lanes]
                if sub == 0:
                    kk = jnp.concatenate([khist[gi][:, lanes], k_ref[0, 0:ATTN_BLOCK, lanes]], axis=0)
                    vv = jnp.concatenate([vhist[gi][:, lanes], v_ref[0, 0:ATTN_BLOCK, lanes]], axis=0)
                else:
                    kk = k_ref[0, lo - N_BACK:lo + ATTN_BLOCK, lanes]
                    vv = v_ref[0, lo - N_BACK:lo + ATTN_BLOCK, lanes]
                o_pair = None
                l_pair = None
                for hp in range(2):
                    s = _dot_nt(q * head_mask[hp], kk) + (bias_first if sub == 0 else bias)
                    m = jnp.max(s, axis=-1, keepdims=True)
                    p = jnp.exp(s - m)
                    den = jnp.sum(p, axis=-1, keepdims=True)
                    o = _dot(p.astype(BF16), vv) / den
                    lse = jnp.broadcast_to(m + jnp.log(den), (ATTN_BLOCK, LANES))
                    o_pair = o if hp == 0 else jnp.where(lane_lo, o_pair, o)
                    l_pair = lse if hp == 0 else jnp.where(lane_lo, l_pair, lse)
                tok_rows = pl.ds(r + d * lo, ATTN_BLOCK, stride=d)
                og_ref[gi, tok_rows, :] = o_pair
                lg_ref[gi, tok_rows, :] = l_pair
        rows = ATTN_TILE // d
        khist[gi][...] = k_ref[0, rows - N_BACK:rows, :]
        vhist[gi][...] = v_ref[0, rows - N_BACK:rows, :]

    l0, l1, l2 = lg_ref[0], lg_ref[1], lg_ref[2]
    lm = jnp.maximum(jnp.maximum(l0, l1), l2)
    w0, w1, w2 = jnp.exp(l0 - lm), jnp.exp(l1 - lm), jnp.exp(l2 - lm)
    out = (w0 * og_ref[0] + w1 * og_ref[1] + w2 * og_ref[2]) / (w0 + w1 + w2)
    o_ref[0] = out.astype(BF16)


def _attention(q_views, k_views, v_views):
    s = q_views[0].shape[1]
    views = list(q_views) + list(k_views) + list(v_views)
    specs = [pl.BlockSpec((1, ATTN_TILE // d, d * LANES), lambda p, t: (p, t, 0)) for d in DILATIONS] * 3
    hist = [pltpu.VMEM((N_BACK, d * LANES), BF16) for d in DILATIONS]
    return pl.pallas_call(
        _attn_body,
        out_shape=jax.ShapeDtypeStruct((ATTN_PAIRS, s, LANES), BF16),
        grid=(ATTN_PAIRS, s // ATTN_TILE),
        in_specs=specs,
        out_specs=pl.BlockSpec((1, ATTN_TILE, LANES), lambda p, t: (p, t, 0)),
        scratch_shapes=hist + hist + [pltpu.VMEM((len(DILATIONS), ATTN_TILE, LANES), F32),
                                      pltpu.VMEM((len(DILATIONS), ATTN_TILE, LANES), F32)],
        compiler_params=_cparams(("arbitrary", "arbitrary")),
        name="dilated_attn",
    )(*views)


def _scan_step(a, b, k, pos):
    ok = pos >= k
    a_sh = pltpu.roll(a, k, 0)
    b_sh = pltpu.roll(b, k, 0)
    return jnp.where(ok, a * a_sh, a), jnp.where(ok, a * b_sh + b, b)


def _lru_body(xr_ref, gr_ref, cw_ref, cb_ref, wa_ref, ba_ref, wx_ref, bx_ref, lam_ref, y_ref,
              ext_ref, carry_ref, a_s, b_s, cin_s, *, tm):
    i = pl.program_id(0)

    @pl.when(i == 0)
    def _():
        ext_ref[0:CONV_PAD] = jnp.zeros((CONV_PAD, LRU_WIDTH), F32)
        carry_ref[...] = jnp.zeros((SUBLANES, LRU_WIDTH), F32)

    @pl.when(i > 0)
    def _():
        ext_ref[0:CONV_PAD] = ext_ref[tm:tm + CONV_PAD]

    ext_ref[CONV_PAD:CONV_PAD + tm] = xr_ref[...]
    xc = _causal_conv(ext_ref[...], cw_ref[...], tm) + cb_ref[...]
    xcb = xc.astype(BF16)
    r = _sigmoid(_dot(xcb, wa_ref[...]) + ba_ref[...])
    ig = _sigmoid(_dot(xcb, wx_ref[...]) + bx_ref[...])
    log_a = -LRU_C * r * _softplus(-lam_ref[...])
    a = jnp.exp(log_a)
    b = jnp.sqrt(-jnp.tanh(log_a) * (a * a + 1.0)) * ig * xc

    pos = lax.broadcasted_iota(jnp.int32, (tm, LRU_WIDTH), 0) & (SUBLANES - 1)
    for k in (1, 2, 4):
        a, b = _scan_step(a, b, k, pos)
    lane_groups = [slice(g * LANES, (g + 1) * LANES) for g in range(LRU_WIDTH // LANES)]
    for g, cols in enumerate(lane_groups):
        a_s[g] = a[:, cols]
        b_s[g] = b[:, cols]
    ng = tm // SUBLANES
    ends = pl.ds(SUBLANES - 1, ng, stride=SUBLANES)
    ae = jnp.concatenate([a_s[g, ends, :] for g in range(len(lane_groups))], axis=1)
    be = jnp.concatenate([b_s[g, ends, :] for g in range(len(lane_groups))], axis=1)
    gpos = lax.broadcasted_iota(jnp.int32, (ng, LRU_WIDTH), 0)
    k = 1
    while k < ng:
        ae, be = _scan_step(ae, be, k, gpos)
        k *= 2
    carry = carry_ref[0:1, :]
    h_end = ae * carry + be
    h_in = jnp.where(gpos >= 1, pltpu.roll(h_end, 1, 0), carry)
    for g, cols in enumerate(lane_groups):
        for j in range(SUBLANES):
            cin_s[g, pl.ds(j, ng, stride=SUBLANES), :] = h_in[:, cols]
    carry_ref[...] = jnp.broadcast_to(h_end[ng - 1:ng, :], (SUBLANES, LRU_WIDTH))
    h = jnp.concatenate([a_s[g] * cin_s[g] + b_s[g] for g in range(len(lane_groups))], axis=1)
    y_ref[...] = (h * _gelu_tanh(gr_ref[...])).astype(BF16)


def _lru(xr, gr, cw, cb, wa, ba, wx, bx, lam, *, tm=1024):
    s = xr.shape[0]
    row = pl.BlockSpec((tm, LRU_WIDTH), lambda i: (i, 0))
    vec = _resident((1, LRU_WIDTH))
    sq = _resident((LRU_WIDTH, LRU_WIDTH))
    return pl.pallas_call(
        functools.partial(_lru_body, tm=tm),
        out_shape=jax.ShapeDtypeStruct((s, LRU_WIDTH), BF16),
        grid=(s // tm,),
        in_specs=[row, row, _resident((CONV_K, LRU_WIDTH)), vec, sq, vec, sq, vec, vec],
        out_specs=row,
        scratch_shapes=[pltpu.VMEM((tm + CONV_PAD, LRU_WIDTH), F32),
                        pltpu.VMEM((SUBLANES, LRU_WIDTH), F32),
                        pltpu.VMEM((LRU_WIDTH // LANES, tm, LANES), F32),
                        pltpu.VMEM((LRU_WIDTH // LANES, tm, LANES), F32),
                        pltpu.VMEM((LRU_WIDTH // LANES, tm, LANES), F32)],
        compiler_params=_cparams(("arbitrary",)),
        name="rglru",
    )(xr, gr, cw, cb, wa, ba, wx, bx, lam)


def _memkv_body(m_ref, g_ref, w_ref, k_ref, v_ref):
    h = _rms(m_ref[...], g_ref[...]).astype(BF16)
    kv = _dot(h, w_ref[...])
    k_ref[...] = kv[:, :D_MODEL].astype(BF16)
    v_ref[...] = kv[:, D_MODEL:].astype(BF16)


def _memkv(mem, g, wkv):
    n = mem.shape[0]
    shp = jax.ShapeDtypeStruct((n, D_MODEL), BF16)
    return pl.pallas_call(
        _memkv_body, out_shape=(shp, shp),
        compiler_params=pltpu.CompilerParams(vmem_limit_bytes=VMEM_LIMIT),
        name="mem_kv",
    )(mem, g, wkv)


def _xattn_body(x_ref, *refs, n_mix, mix_layer, layer):
    y_refs, wmix_hbm = refs[:n_mix], refs[n_mix]
    g_ref, wq_hbm, k_ref, v_ref, wo_hbm, o_ref, wmix_ref, wq_ref, wo_ref, stage, sem = refs[n_mix + 1:]

    @pl.when(pl.program_id(0) == 0)
    def _():
        _stage_weight(wmix_hbm, mix_layer, wmix_ref, stage, sem)
        _stage_weight(wq_hbm, layer, wq_ref, stage, sem)
        _stage_weight(wo_hbm, layer, wo_ref, stage, sem)

    parts = []
    for r in y_refs:
        if len(r.shape) == 3:
            parts.extend(r[j] for j in range(r.shape[0]))
        else:
            parts.append(r[...])
    y = parts[0] if len(parts) == 1 else jnp.concatenate(parts, axis=1)
    x = x_ref[...] + _dot(y, wmix_ref[...])
    h = _rms(x, g_ref[...]).astype(BF16)
    q = (_dot(h, wq_ref[...]) * (XA_HEAD_DIM ** -0.5)).astype(BF16)
    outs = []
    for hh in range(XA_HEADS):
        cols = slice(hh * XA_HEAD_DIM, (hh + 1) * XA_HEAD_DIM)
        s = _dot_nt(q[:, cols], k_ref[:, cols])
        p = jnp.exp(s - jnp.max(s, axis=-1, keepdims=True))
        p = (p / jnp.sum(p, axis=-1, keepdims=True)).astype(BF16)
        outs.append(_dot(p, v_ref[:, cols]).astype(BF16))
    o_ref[...] = x + _dot(jnp.concatenate(outs, axis=1), wo_ref[...])


def _xattn(x, ys, w_mix, mix_layer, g, wq, k, v, wo, layer, *, tm=512):
    s = x.shape[0]
    hbm = pl.BlockSpec(memory_space=pl.ANY)
    assert w_mix.shape[1:] == wq.shape[1:] == wo.shape[1:] == (D_MODEL, D_MODEL)
    wbuf = pltpu.VMEM((D_MODEL, D_MODEL), BF16)
    row = pl.BlockSpec((tm, D_MODEL), lambda i: (i, 0))
    y_specs = []
    for y in ys:
        if y.ndim == 3:
            y_specs.append(pl.BlockSpec((y.shape[0], tm, y.shape[2]), lambda i: (0, i, 0)))
        else:
            y_specs.append(pl.BlockSpec((tm, y.shape[1]), lambda i: (i, 0)))
    return pl.pallas_call(
        functools.partial(_xattn_body, n_mix=len(ys), mix_layer=mix_layer, layer=layer),
        out_shape=jax.ShapeDtypeStruct((s, D_MODEL), F32),
        grid=(s // tm,),
        in_specs=[row] + y_specs + [hbm, _resident((1, D_MODEL)), hbm, _resident(k.shape),
                                    _resident(v.shape), hbm],
        out_specs=row,
        scratch_shapes=[wbuf, wbuf, wbuf,
                        pltpu.VMEM((2, D_MODEL // FF_STAGE_CHUNKS, D_MODEL), F32),
                        pltpu.SemaphoreType.DMA((2,))],
        compiler_params=_cparams(("arbitrary",)),
        name="mix_out_cross_attn",
    )(x, *ys, w_mix, g, wq, k, v, wo)


def _dnprep_body(x_ref, g_ref, w_hbm, wab_ref, wabt_ref, cw_ref,
                 q_ref, k_ref, v_ref, z_ref, ab_ref, abt_ref, ext_ref, w_ref, stage, sem, *, tm, layer):
    i = pl.program_id(0)
    width = 3 * DN_WIDTH

    @pl.when(i == 0)
    def _():
        ext_ref[0:CONV_PAD] = jnp.zeros((CONV_PAD, width), F32)
        _stage_weight(w_hbm, layer, w_ref, stage, sem)

    h = _rms(x_ref[...], g_ref[...]).astype(BF16)
    outs = (q_ref, k_ref, v_ref)
    qkv_blocks = width // MXU_DIM
    z_blocks = DN_WIDTH // MXU_DIM
    for b in range(qkv_blocks):
        if b % (qkv_blocks // z_blocks) == 0:
            zc = (b // (qkv_blocks // z_blocks)) * MXU_DIM
            z_ref[:, zc:zc + MXU_DIM] = _dot(h, w_ref[:, width + zc:width + zc + MXU_DIM])
        c0 = b * MXU_DIM
        cols = slice(c0, c0 + MXU_DIM)
        ext_ref[CONV_PAD:CONV_PAD + tm, cols] = _dot(h, w_ref[:, cols])
        cw = cw_ref[:, cols]
        for r0 in range(0, tm, DN_PREP_ROWS):
            act = _silu(_causal_conv(ext_ref[r0:r0 + DN_PREP_ROWS + CONV_PAD, cols], cw, DN_PREP_ROWS))
            for c1 in range(c0, c0 + MXU_DIM, DN_HEAD_DIM):
                part, hh = divmod(c1 // DN_HEAD_DIM, DN_HEADS)
                u = act[:, c1 - c0:c1 - c0 + DN_HEAD_DIM]
                if part < 2:
                    u = u * lax.rsqrt(jnp.sum(u * u, axis=-1, keepdims=True) + L2_EPS)
                if part == 0:
                    u = u * (DN_HEAD_DIM ** -0.5)
                outs[part][hh, r0:r0 + DN_PREP_ROWS, :] = u.astype(BF16)
    ext_ref[0:CONV_PAD] = ext_ref[tm:tm + CONV_PAD]

    ab_ref[...] = _dot(h, wab_ref[...])
    abt = _dot_nt(wabt_ref[...], h)
    for j in range(tm // DN_CHUNK):
        abt_ref[j] = abt[:, j * DN_CHUNK:(j + 1) * DN_CHUNK]


def _dnprep(x, g, w, layer, wab, wabt, cw, *, tm=512):
    s = x.shape[0]
    row = pl.BlockSpec((tm, D_MODEL), lambda i: (i, 0))
    head = pl.BlockSpec((DN_HEADS, tm, DN_HEAD_DIM), lambda i: (0, i, 0))
    head_shape = jax.ShapeDtypeStruct((DN_HEADS, s, DN_HEAD_DIM), BF16)
    nab = wabt.shape[0]
    return pl.pallas_call(
        functools.partial(_dnprep_body, tm=tm, layer=layer),
        out_shape=(head_shape, head_shape, head_shape,
                   jax.ShapeDtypeStruct((s, DN_WIDTH), F32),
                   jax.ShapeDtypeStruct((s, LANES), F32),
                   jax.ShapeDtypeStruct((s // DN_CHUNK, nab, DN_CHUNK), F32)),
        grid=(s // tm,),
        in_specs=[row, _resident((1, D_MODEL)), pl.BlockSpec(memory_space=pl.ANY),
                  _resident(wab.shape), _resident(wabt.shape), _resident(cw.shape)],
        out_specs=(head, head, head, row,
                   pl.BlockSpec((tm, LANES), lambda i: (i, 0)),
                   pl.BlockSpec((tm // DN_CHUNK, nab, DN_CHUNK), lambda i: (i, 0, 0))),
        scratch_shapes=[pltpu.VMEM((tm + CONV_PAD, 3 * DN_WIDTH), F32)] + _staged_weight_scratch(w.shape[1:]),
        compiler_params=_cparams(("arbitrary",)),
        name="dn_prep",
    )(x, g, w, wab, wabt, cw)


def _dn_body(q_ref, k_ref, v_ref, z_ref, ab_ref, abt_ref, prow_ref, pcol_ref, on_ref, y_ref, state, *, ts):
    c = DN_CHUNK

    @pl.when(pl.program_id(0) == 0)
    def _():
        state[...] = jnp.zeros((DN_HEADS, DN_HEAD_DIM, DN_HEAD_DIM), F32)

    ri = lax.broadcasted_iota(jnp.int32, (c, c), 0)
    ci = lax.broadcasted_iota(jnp.int32, (c, c), 1)
    lower = ri >= ci
    strict = ri > ci
    tri_l = lower.astype(F32)
    tri_u = (ri <= ci).astype(F32)
    eye = (ri == ci).astype(F32)
    diag_blk = (ri // DN_INV_BLOCK) == (ci // DN_INV_BLOCK)
    off_blks = []
    bsz = DN_INV_BLOCK
    while bsz < c:
        off_blks.append(((ri // (2 * bsz)) == (ci // (2 * bsz))) & ((ri // bsz) != (ci // bsz)))
        bsz *= 2
    hi = lax.Precision.HIGHEST
    a_log_row, dtb_row = prow_ref[0:1, :], prow_ref[1:2, :]
    a_log_col, dtb_col = pcol_ref[:, 0:1], pcol_ref[:, 1:2]
    onorm = on_ref[...]

    heads = range(DN_HEADS)

    def step(it, _):
        rows, gcol, beta, decay, egc, kk, qq, kf = [], [], [], [], [], [], [], []
        for u in range(DN_UNROLL):
            ci_ = it * DN_UNROLL + u
            r = pl.ds(pl.multiple_of(ci_ * c, c), c)
            ab = ab_ref[r, :]
            abt = abt_ref[ci_]
            g_col = -jnp.exp(a_log_row) * _softplus(ab + dtb_row)
            gc_col = jnp.dot(tri_l, g_col, precision=hi, preferred_element_type=F32)
            beta_all = _sigmoid(ab)
            g_row = -jnp.exp(a_log_col) * _softplus(abt + dtb_col)
            gc_row = jnp.dot(g_row, tri_u, precision=hi, preferred_element_type=F32)
            for h in heads:
                rows.append(r)
                gcol.append(gc_col[:, h:h + 1])
                beta.append(beta_all[:, DN_HEADS + h:DN_HEADS + h + 1])
                decay.append(jnp.where(lower, jnp.exp(jnp.minimum(gcol[-1] - gc_row[h:h + 1, :], 0.0)), 0.0))
                egc.append(jnp.exp(gcol[-1]))
                kk.append(k_ref[h, r, :])
                qq.append(q_ref[h, r, :])
                kf.append(kk[-1].astype(F32))
        items = range(DN_UNROLL * DN_HEADS)
        kb = [kf[i] * beta[i] for i in items]
        m = [jnp.where(strict, _dot_nt(kb[i].astype(BF16), kk[i]) * decay[i], 0.0) * -1.0 for i in items]
        qk = [jnp.where(lower, _dot_nt(qq[i], kk[i]) * decay[i], 0.0).astype(BF16) for i in items]
        pw = [jnp.where(diag_blk, m[i], 0.0) for i in items]
        t_inv = [eye + pw[i] for i in items]
        for _ in range(int(math.log2(DN_INV_BLOCK)) - 1):
            pwb = [pw[i].astype(BF16) for i in items]
            pw = [_dot(pwb[i], pwb[i]) for i in items]
            t_inv = [t_inv[i] + _dot(t_inv[i].astype(BF16), pw[i].astype(BF16)) for i in items]
        for off_blk in off_blks:
            tb = [t_inv[i].astype(BF16) for i in items]
            x = [_dot(jnp.where(off_blk, m[i], 0.0).astype(BF16), tb[i]).astype(BF16) for i in items]
            t_inv = [t_inv[i] + _dot(tb[i], x[i]) for i in items]
        rhs = [jnp.concatenate([v_ref[i % DN_HEADS, rows[i], :].astype(F32) * beta[i], kb[i] * egc[i]],
                               axis=1).astype(BF16) for i in items]
        sol = [_dot(t_inv[i].astype(BF16), rhs[i]) for i in items]
        qg = [(qq[i].astype(F32) * egc[i]).astype(BF16) for i in items]
        kd, eg_last = [], []
        for i in items:
            glast = gcol[i][c - 1:c, :]
            kd.append((kf[i] * jnp.exp(glast - gcol[i])).astype(BF16))
            eg_last.append(jnp.exp(glast))
        for u in range(DN_UNROLL):
            idx = [u * DN_HEADS + h for h in heads]
            st = [state[h] for h in heads]
            stb = [st[h].astype(BF16) for h in heads]
            v_new = [sol[i][:, :DN_HEAD_DIM] - _dot(sol[i][:, DN_HEAD_DIM:].astype(BF16), stb[h])
                     for h, i in enumerate(idx)]
            vnb = [v.astype(BF16) for v in v_new]
            for h, i in enumerate(idx):
                state[h] = st[h] * eg_last[i] + _dot_tn(kd[i], vnb[h])
            o = [_dot(qg[i], stb[h]) + _dot(qk[i], vnb[h]) for h, i in enumerate(idx)]
            for h, i in enumerate(idx):
                cols = slice(h * DN_HEAD_DIM, (h + 1) * DN_HEAD_DIM)
                on = o[h] * lax.rsqrt(jnp.mean(o[h] * o[h], axis=-1, keepdims=True) + NORM_EPS) * onorm
                y_ref[rows[i], cols] = (on * _silu(z_ref[rows[i], cols])).astype(BF16)
        return 0

    lax.fori_loop(0, ts // (c * DN_UNROLL), step, 0)


def _deltanet(q, k, v, z, ab, abt, prow, pcol, onorm, *, ts=1024):
    s = z.shape[0]
    head = pl.BlockSpec((DN_HEADS, ts, DN_HEAD_DIM), lambda i: (0, i, 0))
    nab = abt.shape[1]
    return pl.pallas_call(
        functools.partial(_dn_body, ts=ts),
        out_shape=jax.ShapeDtypeStruct((s, DN_WIDTH), BF16),
        grid=(s // ts,),
        in_specs=[head, head, head,
                  pl.BlockSpec((ts, DN_WIDTH), lambda i: (i, 0)),
                  pl.BlockSpec((ts, LANES), lambda i: (i, 0)),
                  pl.BlockSpec((ts // DN_CHUNK, nab, DN_CHUNK), lambda i: (i, 0, 0)),
                  _resident(prow.shape), _resident(pcol.shape), _resident(onorm.shape)],
        out_specs=pl.BlockSpec((ts, DN_WIDTH), lambda i: (i, 0)),
        scratch_shapes=[pltpu.VMEM((DN_HEADS, DN_HEAD_DIM, DN_HEAD_DIM), F32)],
        compiler_params=_cparams(("arbitrary",)),
        name="gated_deltanet",
    )(q, k, v, z, ab, abt, prow, pcol, onorm)


def _row(v):
    return v.reshape(1, -1).astype(F32)


def _block_diag(w):
    n, j, k = w.shape
    eye = jnp.eye(n, dtype=w.dtype)
    return (w[:, :, None, :] * eye[:, None, :, None]).reshape(n * j, n * k)


def kernel(x, mem, ffn1_norm, ffn1_w_in, ffn1_w_out, mix_norm, xa_norm, xa_mem_norm, xa_wq, xa_wkv, xa_wo,
           ffn2_norm, ffn2_w_in, ffn2_w_out, ab_w_in, lru_conv_w, lru_conv_b, lru_w_a, lru_b_a, lru_w_x,
           lru_b_x, lru_lambda, ab_w_out, dn_w_in, dn_conv_w, dn_a_log, dn_dt_bias, dn_o_norm, dn_w_out,
           final_norm):
    batch, seq, _ = x.shape
    depth = ffn1_norm.shape[0]
    outs = []
    for bi in range(batch):
        xs = x[bi]
        ms = mem[bi]
        for layer in range(depth):
            j = layer // 2
            last = layer == depth - 1
            xs = _ffn(xs, _row(ffn1_norm[layer]), ffn1_w_in, ffn1_w_out, _row(final_norm), layer,
                      final_norm=False)
            if layer % 2 == 0:
                q, k, v, xr, gr = _proj0(xs, _row(mix_norm[layer]), ab_w_in, j)
                attn = _attention(q, k, v)
                y = _lru(xr, gr, lru_conv_w[j].astype(F32), _row(lru_conv_b[j]),
                         _block_diag(lru_w_a[j]).astype(BF16), _row(lru_b_a[j]),
                         _block_diag(lru_w_x[j]).astype(BF16), _row(lru_b_x[j]), _row(lru_lambda[j]))
                mix_ys, mix_w = (attn, y), ab_w_out
            else:
                w = dn_w_in[j]
                wab = w[:, 4 * DN_WIDTH:]
                n_ab = wab.shape[1]
                wab_pad = jnp.pad(wab, ((0, 0), (0, LANES - n_ab))).astype(BF16)
                q, k, v, z, ab, abt = _dnprep(
                    xs, _row(mix_norm[layer]), dn_w_in, j, wab_pad, wab.T.astype(BF16),
                    dn_conv_w[j].astype(F32))
                pad = jnp.zeros((LANES - DN_HEADS,), F32)
                prow = jnp.stack([jnp.concatenate([dn_a_log[j].astype(F32), pad]),
                                  jnp.concatenate([dn_dt_bias[j].astype(F32), pad])])
                pad2 = jnp.zeros((n_ab - DN_HEADS,), F32)
                pcol = jnp.stack([jnp.concatenate([dn_a_log[j].astype(F32), pad2]),
                                  jnp.concatenate([dn_dt_bias[j].astype(F32), pad2])], axis=1)
                y = _deltanet(q, k, v, z, ab, abt, prow, pcol, _row(dn_o_norm[j]))
                mix_ys, mix_w = (y,), dn_w_out
            mk, mv = _memkv(ms, _row(xa_mem_norm[layer]), xa_wkv[layer].astype(BF16))
            xs = _xattn(xs, mix_ys, mix_w, j, _row(xa_norm[layer]), xa_wq, mk, mv, xa_wo, layer)
            xs = _ffn(xs, _row(ffn2_norm[layer]), ffn2_w_in, ffn2_w_out, _row(final_norm), layer,
                      final_norm=last)
        outs.append(xs)
    return jnp.stack(outs, axis=0)
```

```python
import functools
import math

import jax
import jax.numpy as jnp
from jax import lax
from jax.experimental import pallas as pl
from jax.experimental.pallas import tpu as pltpu

F32 = jnp.float32
BF16 = jnp.bfloat16

NORM_EPS = 1e-6
L2_EPS = 1e-6
LANES = 128
SUBLANES = 8

D_MODEL = 1024
D_FF = 2816
MXU_DIM = 256
FF_CHUNKS = (6 * MXU_DIM, 5 * MXU_DIM)
assert sum(FF_CHUNKS) == D_FF
FF_STAGE_CHUNKS = 8
ATTN_HEADS = 8
ATTN_HEAD_DIM = 64
ATTN_WIDTH = 512
ATTN_PAIRS = ATTN_WIDTH // LANES
DILATIONS = (1, 4, 16)
N_BACK = 128
ATTN_BLOCK = 128
ATTN_TILE = 2048
LRU_WIDTH = 512
LRU_C = 8.0
CONV_K = 4
CONV_PAD = SUBLANES
DN_HEADS = 8
DN_HEAD_DIM = 128
DN_WIDTH = 1024
DN_CHUNK = 128
DN_INV_BLOCK = 16
DN_UNROLL = 4
DN_PREP_ROWS = 64
XA_HEADS = 4
XA_HEAD_DIM = 256
NEG_BIG = -1e30
LOG2_E = math.log2(math.e)
VMEM_LIMIT = 56 * 1024 * 1024


def _cparams(sem):
    return pltpu.CompilerParams(dimension_semantics=sem, vmem_limit_bytes=VMEM_LIMIT)


def _resident(shape, layer=None):
    if layer is None:
        nd = len(shape)
        return pl.BlockSpec(shape, lambda *_: (0,) * nd, pipeline_mode=pl.Buffered(1))
    nd = len(shape) - 1
    return pl.BlockSpec((None,) + tuple(shape[1:]), lambda *_: (layer,) + (0,) * nd,
                        pipeline_mode=pl.Buffered(1))


def _dot(a, b):
    return jnp.dot(a, b, preferred_element_type=F32)


def _dot_nt(a, b):
    return lax.dot_general(a, b, (((1,), (1,)), ((), ())), preferred_element_type=F32)


def _dot_tn(a, b):
    return lax.dot_general(a, b, (((0,), (0,)), ((), ())), preferred_element_type=F32)


def _rms(x, g):
    return x * lax.rsqrt(jnp.mean(x * x, axis=-1, keepdims=True) + NORM_EPS) * g


def _sigmoid(x):
    return 1.0 / (1.0 + jnp.exp(-x))


def _silu(x):
    return x * _sigmoid(x)


def _softplus(x):
    return jnp.maximum(x, 0.0) + jnp.log1p(jnp.exp(-jnp.abs(x)))


def _causal_conv(ext, w, tm):
    y = w[CONV_K - 1:CONV_K, :] * ext[CONV_PAD:CONV_PAD + tm]
    for shift in range(1, CONV_K):
        k = CONV_K - 1 - shift
        y = y + w[k:k + 1, :] * pltpu.roll(ext, shift, 0)[CONV_PAD:CONV_PAD + tm]
    return y


def _gelu_tanh(x):
    c = math.sqrt(2.0 / math.pi)
    return 0.5 * x * (1.0 + jnp.tanh(c * (x + 0.044715 * (x * x * x))))


def _stage_weight(src_hbm, layer, dst, stage, sem):
    rows = stage.shape[1]
    n = dst.shape[0] // rows

    def copy(c):
        return pltpu.make_async_copy(src_hbm.at[layer, pl.ds(c * rows, rows), :], stage.at[c % 2], sem.at[c % 2])

    copy(0).start()
    for c in range(n):
        if c + 1 < n:
            copy(c + 1).start()
        copy(c).wait()
        dst[c * rows:(c + 1) * rows, :] = stage[c % 2].astype(BF16)


def _staged_weight_scratch(shape):
    rows, cols = shape
    return [pltpu.VMEM((rows, cols), BF16), pltpu.VMEM((2, rows // FF_STAGE_CHUNKS, cols), F32),
            pltpu.SemaphoreType.DMA((2,))]


def _ffn_body(x_ref, g_ref, win_hbm, wout_hbm, fg_ref, o_ref, win_ref, sin, sem_in, wout_ref, sout, sem_out,
              *, layer, final_norm):
    @pl.when(pl.program_id(0) == 0)
    def _():
        _stage_weight(win_hbm, layer, win_ref, sin, sem_in)
        _stage_weight(wout_hbm, layer, wout_ref, sout, sem_out)

    x = x_ref[...]
    h = _rms(x, g_ref[...]).astype(BF16)
    acc = jnp.zeros_like(x)
    lo = 0
    for width in FF_CHUNKS:
        gate = _dot(h, win_ref[:, lo:lo + width])
        up = _dot(h, win_ref[:, D_FF + lo:D_FF + lo + width])
        a = (_silu(gate) * up).astype(BF16)
        acc = acc + _dot(a, wout_ref[lo:lo + width, :])
        lo += width
    y = x + 0.5 * acc
    if final_norm:
        y = _rms(y, fg_ref[...])
    o_ref[...] = y


def _ffn(x, g, w_in, w_out, fg, layer, *, final_norm, tm=512):
    s = x.shape[0]
    row = pl.BlockSpec((tm, D_MODEL), lambda i: (i, 0))
    hbm = pl.BlockSpec(memory_space=pl.ANY)
    return pl.pallas_call(
        functools.partial(_ffn_body, layer=layer, final_norm=final_norm),
        out_shape=jax.ShapeDtypeStruct((s, D_MODEL), F32),
        grid=(s // tm,),
        in_specs=[row, _resident((1, D_MODEL)), hbm, hbm, _resident((1, D_MODEL))],
        out_specs=row,
        scratch_shapes=_staged_weight_scratch(w_in.shape[1:]) + _staged_weight_scratch(w_out.shape[1:]),
        compiler_params=_cparams(("arbitrary",)),
        name="ffn_final" if final_norm else "ffn",
    )(x, g, w_in, w_out, fg)


def _proj0_body(x_ref, g_ref, w_hbm, *refs, tm, layer):
    nd = len(DILATIONS)
    qkv_refs = (refs[0:nd], refs[nd:2 * nd], refs[2 * nd:3 * nd])
    xr_ref, gr_ref = refs[3 * nd], refs[3 * nd + 1]
    w_ref, stage, sem = refs[3 * nd + 2:]

    @pl.when(pl.program_id(0) == 0)
    def _():
        _stage_weight(w_hbm, layer, w_ref, stage, sem)

    h = _rms(x_ref[...], g_ref[...]).astype(BF16)
    p = _dot(h, w_ref[...])
    xr_ref[...] = p[:, 3 * ATTN_WIDTH:3 * ATTN_WIDTH + LRU_WIDTH]
    gr_ref[...] = p[:, 3 * ATTN_WIDTH + LRU_WIDTH:]
    scale = ATTN_HEAD_DIM ** -0.5 * LOG2_E
    qkv = jnp.concatenate([p[:, :ATTN_WIDTH] * scale, p[:, ATTN_WIDTH:3 * ATTN_WIDTH]], axis=1).astype(BF16)

    ri = lax.broadcasted_iota(jnp.int32, (MXU_DIM, MXU_DIM), 0)
    ci = lax.broadcasted_iota(jnp.int32, (MXU_DIM, MXU_DIM), 1)
    for gi, d in enumerate(DILATIONS):
        per = MXU_DIM // d
        perm = (ci == (ri % per) * d + ri // per).astype(BF16) if d > 1 else None
        for c in range(tm // MXU_DIM):
            chunk = qkv[c * MXU_DIM:(c + 1) * MXU_DIM]
            if perm is not None:
                chunk = _dot(perm, chunk).astype(BF16)
            for r in range(d):
                piece = chunk[r * per:(r + 1) * per]
                for a in range(3):
                    for j in range(ATTN_PAIRS):
                        col = a * ATTN_WIDTH + j * LANES
                        qkv_refs[a][gi][j, c * per:(c + 1) * per, r * LANES:(r + 1) * LANES] = (
                            piece[:, col:col + LANES])


def _proj0(x, g, w, layer, *, tm=1024):
    s = x.shape[0]
    nd = len(DILATIONS)
    row = pl.BlockSpec((tm, D_MODEL), lambda i: (i, 0))
    half = pl.BlockSpec((tm, LRU_WIDTH), lambda i: (i, 0))
    view_shapes = [jax.ShapeDtypeStruct((ATTN_PAIRS, s // d, d * LANES), BF16) for d in DILATIONS]
    view_specs = [pl.BlockSpec((ATTN_PAIRS, tm // d, d * LANES), lambda i: (0, i, 0)) for d in DILATIONS]
    half_shape = jax.ShapeDtypeStruct((s, LRU_WIDTH), F32)
    outs = pl.pallas_call(
        functools.partial(_proj0_body, tm=tm, layer=layer),
        out_shape=tuple(view_shapes * 3) + (half_shape, half_shape),
        grid=(s // tm,),
        in_specs=[row, _resident((1, D_MODEL)), pl.BlockSpec(memory_space=pl.ANY)],
        out_specs=tuple(view_specs * 3) + (half, half),
        scratch_shapes=_staged_weight_scratch(w.shape[1:]),
        compiler_params=_cparams(("arbitrary",)),
        name="proj0",
    )(x, g, w)
    return outs[0:nd], outs[nd:2 * nd], outs[2 * nd:3 * nd], outs[3 * nd], outs[3 * nd + 1]


def _attn_body(*refs):
    nd = len(DILATIONS)
    q_refs, k_refs, v_refs = refs[0:nd], refs[nd:2 * nd], refs[2 * nd:3 * nd]
    o_ref = refs[3 * nd]
    khist, vhist = refs[3 * nd + 1:4 * nd + 1], refs[4 * nd + 1:5 * nd + 1]
    og_ref, lg_ref = refs[5 * nd + 1], refs[5 * nd + 2]
    t = pl.program_id(1)

    @pl.when(t == 0)
    def _():
        for h_ref in khist + vhist:
            h_ref[...] = jnp.zeros(h_ref.shape, BF16)

    nk = N_BACK + ATTN_BLOCK
    qi = lax.broadcasted_iota(jnp.int32, (ATTN_BLOCK, nk), 0)
    kj = lax.broadcasted_iota(jnp.int32, (ATTN_BLOCK, nk), 1)
    dist = qi + N_BACK - kj
    band = (dist >= 0) & (dist <= N_BACK)
    bias = jnp.where(band, 0.0, NEG_BIG)
    bias_first = jnp.where(band & (kj >= jnp.where(t == 0, N_BACK, 0)), 0.0, NEG_BIG)
    lane = lax.broadcasted_iota(jnp.int32, (1, LANES), 1)
    head_mask = [(lane < ATTN_HEAD_DIM).astype(BF16), (lane >= ATTN_HEAD_DIM).astype(BF16)]
    lane_lo = lax.broadcasted_iota(jnp.int32, (ATTN_BLOCK, LANES), 1) < ATTN_HEAD_DIM

    for gi, d in enumerate(DILATIONS):
        q_ref, k_ref, v_ref = q_refs[gi], k_refs[gi], v_refs[gi]
        for r in range(d):
            lanes = slice(r * LANES, (r + 1) * LANES)
            for sub in range(ATTN_TILE // d // ATTN_BLOCK):
                lo = ATTN_BLOCK * sub
                q = q_ref[0, lo:lo + ATTN_BLOCK, lanes]
                if sub == 0:
                    kk = jnp.concatenate([khist[gi][:, lanes], k_ref[0, 0:ATTN_BLOCK, lanes]], axis=0)
                    vv = jnp.concatenate([vhist[gi][:, lanes], v_ref[0, 0:ATTN_BLOCK, lanes]], axis=0)
                else:
                    kk = k_ref[0, lo - N_BACK:lo + ATTN_BLOCK, lanes]
                    vv = v_ref[0, lo - N_BACK:lo + ATTN_BLOCK, lanes]
                o_pair = None
                l_pair = None
                for hp in range(2):
                    s = _dot_nt(q * head_mask[hp], kk) + (bias_first if sub == 0 else bias)
                    m = jnp.max(s, axis=-1, keepdims=True)
                    p = jnp.exp2(s - m)
                    den = jnp.sum(p, axis=-1, keepdims=True)
                    o = _dot(p.astype(BF16), vv) / den
                    lse = jnp.broadcast_to(m + jnp.log2(den), (ATTN_BLOCK, LANES))
                    o_pair = o if hp == 0 else jnp.where(lane_lo, o_pair, o)
                    l_pair = lse if hp == 0 else jnp.where(lane_lo, l_pair, lse)
                tok_rows = pl.ds(r + d * lo, ATTN_BLOCK, stride=d)
                og_ref[gi, tok_rows, :] = o_pair
                lg_ref[gi, tok_rows, :] = l_pair
        rows = ATTN_TILE // d
        khist[gi][...] = k_ref[0, rows - N_BACK:rows, :]
        vhist[gi][...] = v_ref[0, rows - N_BACK:rows, :]

    l0, l1, l2 = lg_ref[0], lg_ref[1], lg_ref[2]
    lm = jnp.maximum(jnp.maximum(l0, l1), l2)
    w0, w1, w2 = jnp.exp2(l0 - lm), jnp.exp2(l1 - lm), jnp.exp2(l2 - lm)
    out = (w0 * og_ref[0] + w1 * og_ref[1] + w2 * og_ref[2]) / (w0 + w1 + w2)
    o_ref[0] = out.astype(BF16)


def _attention(q_views, k_views, v_views):
    s = q_views[0].shape[1]
    views = list(q_views) + list(k_views) + list(v_views)
    specs = [pl.BlockSpec((1, ATTN_TILE // d, d * LANES), lambda p, t: (p, t, 0)) for d in DILATIONS] * 3
    hist = [pltpu.VMEM((N_BACK, d * LANES), BF16) for d in DILATIONS]
    return pl.pallas_call(
        _attn_body,
        out_shape=jax.ShapeDtypeStruct((ATTN_PAIRS, s, LANES), BF16),
        grid=(ATTN_PAIRS, s // ATTN_TILE),
        in_specs=specs,
        out_specs=pl.BlockSpec((1, ATTN_TILE, LANES), lambda p, t: (p, t, 0)),
        scratch_shapes=hist + hist + [pltpu.VMEM((len(DILATIONS), ATTN_TILE, LANES), F32),
                                      pltpu.VMEM((len(DILATIONS), ATTN_TILE, LANES), F32)],
        compiler_params=_cparams(("arbitrary", "arbitrary")),
        name="dilated_attn",
    )(*views)


def _scan_step(a, b, k, pos):
    ok = pos >= k
    a_sh = pltpu.roll(a, k, 0)
    b_sh = pltpu.roll(b, k, 0)
    return jnp.where(ok, a * a_sh, a), jnp.where(ok, a * b_sh + b, b)


def _lru_body(xr_ref, gr_ref, cw_ref, cb_ref, wa_ref, ba_ref, wx_ref, bx_ref, lam_ref, y_ref,
              ext_ref, carry_ref, a_s, b_s, cin_s, *, tm):
    i = pl.program_id(0)

    @pl.when(i == 0)
    def _():
        ext_ref[0:CONV_PAD] = jnp.zeros((CONV_PAD, LRU_WIDTH), F32)
        carry_ref[...] = jnp.zeros((SUBLANES, LRU_WIDTH), F32)

    @pl.when(i > 0)
    def _():
        ext_ref[0:CONV_PAD] = ext_ref[tm:tm + CONV_PAD]

    ext_ref[CONV_PAD:CONV_PAD + tm] = xr_ref[...]
    xc = _causal_conv(ext_ref[...], cw_ref[...], tm) + cb_ref[...]
    xcb = xc.astype(BF16)
    r = _sigmoid(_dot(xcb, wa_ref[...]) + ba_ref[...])
    ig = _sigmoid(_dot(xcb, wx_ref[...]) + bx_ref[...])
    log_a = -LRU_C * r * _softplus(-lam_ref[...])
    a = jnp.exp(log_a)
    b = jnp.sqrt(-jnp.tanh(log_a) * (a * a + 1.0)) * ig * xc

    pos = lax.broadcasted_iota(jnp.int32, (tm, LRU_WIDTH), 0) & (SUBLANES - 1)
    for k in (1, 2, 4):
        a, b = _scan_step(a, b, k, pos)
    lane_groups = [slice(g * LANES, (g + 1) * LANES) for g in range(LRU_WIDTH // LANES)]
    for g, cols in enumerate(lane_groups):
        a_s[g] = a[:, cols]
        b_s[g] = b[:, cols]
    ng = tm // SUBLANES
    ends = pl.ds(SUBLANES - 1, ng, stride=SUBLANES)
    ae = jnp.concatenate([a_s[g, ends, :] for g in range(len(lane_groups))], axis=1)
    be = jnp.concatenate([b_s[g, ends, :] for g in range(len(lane_groups))], axis=1)
    gpos = lax.broadcasted_iota(jnp.int32, (ng, LRU_WIDTH), 0)
    k = 1
    while k < ng:
        ae, be = _scan_step(ae, be, k, gpos)
        k *= 2
    carry = carry_ref[0:1, :]
    h_end = ae * carry + be
    h_in = jnp.where(gpos >= 1, pltpu.roll(h_end, 1, 0), carry)
    for g, cols in enumerate(lane_groups):
        for j in range(SUBLANES):
            cin_s[g, pl.ds(j, ng, stride=SUBLANES), :] = h_in[:, cols]
    carry_ref[...] = jnp.broadcast_to(h_end[ng - 1:ng, :], (SUBLANES, LRU_WIDTH))
    h = jnp.concatenate([a_s[g] * cin_s[g] + b_s[g] for g in range(len(lane_groups))], axis=1)
    y_ref[...] = (h * _gelu_tanh(gr_ref[...])).astype(BF16)


def _lru(xr, gr, cw, cb, wa, ba, wx, bx, lam, *, tm=1024):
    s = xr.shape[0]
    row = pl.BlockSpec((tm, LRU_WIDTH), lambda i: (i, 0))
    vec = _resident((1, LRU_WIDTH))
    sq = _resident((LRU_WIDTH, LRU_WIDTH))
    return pl.pallas_call(
        functools.partial(_lru_body, tm=tm),
        out_shape=jax.ShapeDtypeStruct((s, LRU_WIDTH), BF16),
        grid=(s // tm,),
        in_specs=[row, row, _resident((CONV_K, LRU_WIDTH)), vec, sq, vec, sq, vec, vec],
        out_specs=row,
        scratch_shapes=[pltpu.VMEM((tm + CONV_PAD, LRU_WIDTH), F32),
                        pltpu.VMEM((SUBLANES, LRU_WIDTH), F32),
                        pltpu.VMEM((LRU_WIDTH // LANES, tm, LANES), F32),
                        pltpu.VMEM((LRU_WIDTH // LANES, tm, LANES), F32),
                        pltpu.VMEM((LRU_WIDTH // LANES, tm, LANES), F32)],
        compiler_params=_cparams(("arbitrary",)),
        name="rglru",
    )(xr, gr, cw, cb, wa, ba, wx, bx, lam)


def _memkv_body(m_ref, g_ref, w_ref, k_ref, v_ref):
    h = _rms(m_ref[...], g_ref[...]).astype(BF16)
    kv = _dot(h, w_ref[...])
    k_ref[...] = kv[:, :D_MODEL].astype(BF16)
    v_ref[...] = kv[:, D_MODEL:].astype(BF16)


def _memkv(mem, g, wkv):
    n = mem.shape[0]
    shp = jax.ShapeDtypeStruct((n, D_MODEL), BF16)
    return pl.pallas_call(
        _memkv_body, out_shape=(shp, shp),
        compiler_params=pltpu.CompilerParams(vmem_limit_bytes=VMEM_LIMIT),
        name="mem_kv",
    )(mem, g, wkv)


def _xattn_body(x_ref, *refs, n_mix, mix_layer, layer):
    y_refs, wmix_hbm = refs[:n_mix], refs[n_mix]
    g_ref, wq_hbm, k_ref, v_ref, wo_hbm, o_ref, wmix_ref, wq_ref, wo_ref, stage, sem = refs[n_mix + 1:]

    @pl.when(pl.program_id(0) == 0)
    def _():
        _stage_weight(wmix_hbm, mix_layer, wmix_ref, stage, sem)
        _stage_weight(wq_hbm, layer, wq_ref, stage, sem)
        _stage_weight(wo_hbm, layer, wo_ref, stage, sem)

    parts = []
    for r in y_refs:
        if len(r.shape) == 3:
            parts.extend(r[j] for j in range(r.shape[0]))
        else:
            parts.append(r[...])
    y = parts[0] if len(parts) == 1 else jnp.concatenate(parts, axis=1)
    x = x_ref[...] + _dot(y, wmix_ref[...])
    h = _rms(x, g_ref[...]).astype(BF16)
    q = (_dot(h, wq_ref[...]) * (XA_HEAD_DIM ** -0.5)).astype(BF16)
    outs = []
    for hh in range(XA_HEADS):
        cols = slice(hh * XA_HEAD_DIM, (hh + 1) * XA_HEAD_DIM)
        s = _dot_nt(q[:, cols], k_ref[:, cols])
        p = jnp.exp(s - jnp.max(s, axis=-1, keepdims=True))
        p = (p / jnp.sum(p, axis=-1, keepdims=True)).astype(BF16)
        outs.append(_dot(p, v_ref[:, cols]).astype(BF16))
    o_ref[...] = x + _dot(jnp.concatenate(outs, axis=1), wo_ref[...])


def _xattn(x, ys, w_mix, mix_layer, g, wq, k, v, wo, layer, *, tm=1024):
    s = x.shape[0]
    hbm = pl.BlockSpec(memory_space=pl.ANY)
    assert w_mix.shape[1:] == wq.shape[1:] == wo.shape[1:] == (D_MODEL, D_MODEL)
    wbuf = pltpu.VMEM((D_MODEL, D_MODEL), BF16)
    row = pl.BlockSpec((tm, D_MODEL), lambda i: (i, 0))
    y_specs = []
    for y in ys:
        if y.ndim == 3:
            y_specs.append(pl.BlockSpec((y.shape[0], tm, y.shape[2]), lambda i: (0, i, 0)))
        else:
            y_specs.append(pl.BlockSpec((tm, y.shape[1]), lambda i: (i, 0)))
    return pl.pallas_call(
        functools.partial(_xattn_body, n_mix=len(ys), mix_layer=mix_layer, layer=layer),
        out_shape=jax.ShapeDtypeStruct((s, D_MODEL), F32),
        grid=(s // tm,),
        in_specs=[row] + y_specs + [hbm, _resident((1, D_MODEL)), hbm, _resident(k.shape),
                                    _resident(v.shape), hbm],
        out_specs=row,
        scratch_shapes=[wbuf, wbuf, wbuf,
                        pltpu.VMEM((2, D_MODEL // FF_STAGE_CHUNKS, D_MODEL), F32),
                        pltpu.SemaphoreType.DMA((2,))],
        compiler_params=_cparams(("arbitrary",)),
        name="mix_out_cross_attn",
    )(x, *ys, w_mix, g, wq, k, v, wo)


def _dnprep_body(x_ref, g_ref, w_hbm, wab_ref, wabt_ref, cw_ref,
                 q_ref, k_ref, v_ref, z_ref, ab_ref, abt_ref, ext_ref, w_ref, stage, sem, *, tm, layer):
    i = pl.program_id(0)
    width = 3 * DN_WIDTH

    @pl.when(i == 0)
    def _():
        ext_ref[0:CONV_PAD] = jnp.zeros((CONV_PAD, width), F32)
        _stage_weight(w_hbm, layer, w_ref, stage, sem)

    h = _rms(x_ref[...], g_ref[...]).astype(BF16)
    outs = (q_ref, k_ref, v_ref)
    qkv_blocks = width // MXU_DIM
    z_blocks = DN_WIDTH // MXU_DIM
    for b in range(qkv_blocks):
        if b % (qkv_blocks // z_blocks) == 0:
            zc = (b // (qkv_blocks // z_blocks)) * MXU_DIM
            z_ref[:, zc:zc + MXU_DIM] = _dot(h, w_ref[:, width + zc:width + zc + MXU_DIM])
        c0 = b * MXU_DIM
        cols = slice(c0, c0 + MXU_DIM)
        ext_ref[CONV_PAD:CONV_PAD + tm, cols] = _dot(h, w_ref[:, cols])
        cw = cw_ref[:, cols]
        for r0 in range(0, tm, DN_PREP_ROWS):
            act = _silu(_causal_conv(ext_ref[r0:r0 + DN_PREP_ROWS + CONV_PAD, cols], cw, DN_PREP_ROWS))
            for c1 in range(c0, c0 + MXU_DIM, DN_HEAD_DIM):
                part, hh = divmod(c1 // DN_HEAD_DIM, DN_HEADS)
                u = act[:, c1 - c0:c1 - c0 + DN_HEAD_DIM]
                if part < 2:
                    u = u * lax.rsqrt(jnp.sum(u * u, axis=-1, keepdims=True) + L2_EPS)
                if part == 0:
                    u = u * (DN_HEAD_DIM ** -0.5)
                outs[part][hh, r0:r0 + DN_PREP_ROWS, :] = u.astype(BF16)
    ext_ref[0:CONV_PAD] = ext_ref[tm:tm + CONV_PAD]

    ab_ref[...] = _dot(h, wab_ref[...])
    abt = _dot_nt(wabt_ref[...], h)
    for j in range(tm // DN_CHUNK):
        abt_ref[j] = abt[:, j * DN_CHUNK:(j + 1) * DN_CHUNK]


def _dnprep(x, g, w, layer, wab, wabt, cw, *, tm=256):
    s = x.shape[0]
    row = pl.BlockSpec((tm, D_MODEL), lambda i: (i, 0))
    head = pl.BlockSpec((DN_HEADS, tm, DN_HEAD_DIM), lambda i: (0, i, 0))
    head_shape = jax.ShapeDtypeStruct((DN_HEADS, s, DN_HEAD_DIM), BF16)
    nab = wabt.shape[0]
    return pl.pallas_call(
        functools.partial(_dnprep_body, tm=tm, layer=layer),
        out_shape=(head_shape, head_shape, head_shape,
                   jax.ShapeDtypeStruct((s, DN_WIDTH), F32),
                   jax.ShapeDtypeStruct((s, LANES), F32),
                   jax.ShapeDtypeStruct((s // DN_CHUNK, nab, DN_CHUNK), F32)),
        grid=(s // tm,),
        in_specs=[row, _resident((1, D_MODEL)), pl.BlockSpec(memory_space=pl.ANY),
                  _resident(wab.shape), _resident(wabt.shape), _resident(cw.shape)],
        out_specs=(head, head, head, row,
                   pl.BlockSpec((tm, LANES), lambda i: (i, 0)),
                   pl.BlockSpec((tm // DN_CHUNK, nab, DN_CHUNK), lambda i: (i, 0, 0))),
        scratch_shapes=[pltpu.VMEM((tm + CONV_PAD, 3 * DN_WIDTH), F32)] + _staged_weight_scratch(w.shape[1:]),
        compiler_params=_cparams(("arbitrary",)),
        name="dn_prep",
    )(x, g, w, wab, wabt, cw)


def _dn_body(q_ref, k_ref, v_ref, z_ref, ab_ref, abt_ref, prow_ref, pcol_ref, on_ref, y_ref, state, *, ts):
    c = DN_CHUNK

    @pl.when(pl.program_id(0) == 0)
    def _():
        state[...] = jnp.zeros((DN_HEADS, DN_HEAD_DIM, DN_HEAD_DIM), F32)

    ri = lax.broadcasted_iota(jnp.int32, (c, c), 0)
    ci = lax.broadcasted_iota(jnp.int32, (c, c), 1)
    lower = ri >= ci
    strict = ri > ci
    tri_l = lower.astype(F32)
    tri_u = (ri <= ci).astype(F32)
    eye = (ri == ci).astype(F32)
    diag_blk = (ri // DN_INV_BLOCK) == (ci // DN_INV_BLOCK)
    off_blks = []
    bsz = DN_INV_BLOCK
    while bsz < c:
        off_blks.append(((ri // (2 * bsz)) == (ci // (2 * bsz))) & ((ri // bsz) != (ci // bsz)))
        bsz *= 2
    hi = lax.Precision.HIGHEST
    a_log_row, dtb_row = prow_ref[0:1, :], prow_ref[1:2, :]
    a_log_col, dtb_col = pcol_ref[:, 0:1], pcol_ref[:, 1:2]
    onorm = on_ref[...]

    heads = range(DN_HEADS)

    def step(it, _):
        rows, gcol, beta, decay, egc, kk, qq, kf = [], [], [], [], [], [], [], []
        for u in range(DN_UNROLL):
            ci_ = it * DN_UNROLL + u
            r = pl.ds(pl.multiple_of(ci_ * c, c), c)
            ab = ab_ref[r, :]
            abt = abt_ref[ci_]
            g_col = -jnp.exp(a_log_row) * _softplus(ab + dtb_row)
            gc_col = jnp.dot(tri_l, g_col, precision=hi, preferred_element_type=F32)
            beta_all = _sigmoid(ab)
            g_row = -jnp.exp(a_log_col) * _softplus(abt + dtb_col)
            gc_row = jnp.dot(g_row, tri_u, precision=hi, preferred_element_type=F32)
            for h in heads:
                rows.append(r)
                gcol.append(gc_col[:, h:h + 1])
                beta.append(beta_all[:, DN_HEADS + h:DN_HEADS + h + 1])
                decay.append(jnp.where(lower, jnp.exp(jnp.minimum(gcol[-1] - gc_row[h:h + 1, :], 0.0)), 0.0))
                egc.append(jnp.exp(gcol[-1]))
                kk.append(k_ref[h, r, :])
                qq.append(q_ref[h, r, :])
                kf.append(kk[-1].astype(F32))
        items = range(DN_UNROLL * DN_HEADS)
        kb = [kf[i] * beta[i] for i in items]
        m = [jnp.where(strict, _dot_nt(kb[i].astype(BF16), kk[i]) * decay[i], 0.0) * -1.0 for i in items]
        qk = [jnp.where(lower, _dot_nt(qq[i], kk[i]) * decay[i], 0.0).astype(BF16) for i in items]
        pw = [jnp.where(diag_blk, m[i], 0.0) for i in items]
        t_inv = [eye + pw[i] for i in items]
        for _ in range(int(math.log2(DN_INV_BLOCK)) - 1):
            pwb = [pw[i].astype(BF16) for i in items]
            pw = [_dot(pwb[i], pwb[i]) for i in items]
            t_inv = [t_inv[i] + _dot(t_inv[i].astype(BF16), pw[i].astype(BF16)) for i in items]
        for off_blk in off_blks:
            tb = [t_inv[i].astype(BF16) for i in items]
            x = [_dot(jnp.where(off_blk, m[i], 0.0).astype(BF16), tb[i]).astype(BF16) for i in items]
            t_inv = [t_inv[i] + _dot(tb[i], x[i]) for i in items]
        rhs = [jnp.concatenate([v_ref[i % DN_HEADS, rows[i], :].astype(F32) * beta[i], kb[i] * egc[i]],
                               axis=1).astype(BF16) for i in items]
        sol = [_dot(t_inv[i].astype(BF16), rhs[i]) for i in items]
        qg = [(qq[i].astype(F32) * egc[i]).astype(BF16) for i in items]
        kd, eg_last = [], []
        for i in items:
            glast = gcol[i][c - 1:c, :]
            kd.append((kf[i] * jnp.exp(glast - gcol[i])).astype(BF16))
            eg_last.append(jnp.exp(glast))
        for u in range(DN_UNROLL):
            idx = [u * DN_HEADS + h for h in heads]
            st = [state[h] for h in heads]
            stb = [st[h].astype(BF16) for h in heads]
            v_new = [sol[i][:, :DN_HEAD_DIM] - _dot(sol[i][:, DN_HEAD_DIM:].astype(BF16), stb[h])
                     for h, i in enumerate(idx)]
            vnb = [v.astype(BF16) for v in v_new]
            for h, i in enumerate(idx):
                state[h] = st[h] * eg_last[i] + _dot_tn(kd[i], vnb[h])
            o = [_dot(qg[i], stb[h]) + _dot(qk[i], vnb[h]) for h, i in enumerate(idx)]
            for h, i in enumerate(idx):
                cols = slice(h * DN_HEAD_DIM, (h + 1) * DN_HEAD_DIM)
                on = o[h] * lax.rsqrt(jnp.mean(o[h] * o[h], axis=-1, keepdims=True) + NORM_EPS) * onorm
                y_ref[rows[i], cols] = (on * _silu(z_ref[rows[i], cols])).astype(BF16)
        return 0

    lax.fori_loop(0, ts // (c * DN_UNROLL), step, 0)


def _deltanet(q, k, v, z, ab, abt, prow, pcol, onorm, *, ts=1024):
    s = z.shape[0]
    head = pl.BlockSpec((DN_HEADS, ts, DN_HEAD_DIM), lambda i: (0, i, 0))
    nab = abt.shape[1]
    return pl.pallas_call(
        functools.partial(_dn_body, ts=ts),
        out_shape=jax.ShapeDtypeStruct((s, DN_WIDTH), BF16),
        grid=(s // ts,),
        in_specs=[head, head, head,
                  pl.BlockSpec((ts, DN_WIDTH), lambda i: (i, 0)),
                  pl.BlockSpec((ts, LANES), lambda i: (i, 0)),
                  pl.BlockSpec((ts // DN_CHUNK, nab, DN_CHUNK), lambda i: (i, 0, 0)),
                  _resident(prow.shape), _resident(pcol.shape), _resident(onorm.shape)],
        out_specs=pl.BlockSpec((ts, DN_WIDTH), lambda i: (i, 0)),
        scratch_shapes=[pltpu.VMEM((DN_HEADS, DN_HEAD_DIM, DN_HEAD_DIM), F32)],
        compiler_params=_cparams(("arbitrary",)),
        name="gated_deltanet",
    )(q, k, v, z, ab, abt, prow, pcol, onorm)


def _row(v):
    return v.reshape(1, -1).astype(F32)


def _block_diag(w):
    n, j, k = w.shape
    eye = jnp.eye(n, dtype=w.dtype)
    return (w[:, :, None, :] * eye[:, None, :, None]).reshape(n * j, n * k)


def kernel(x, mem, ffn1_norm, ffn1_w_in, ffn1_w_out, mix_norm, xa_norm, xa_mem_norm, xa_wq, xa_wkv, xa_wo,
           ffn2_norm, ffn2_w_in, ffn2_w_out, ab_w_in, lru_conv_w, lru_conv_b, lru_w_a, lru_b_a, lru_w_x,
           lru_b_x, lru_lambda, ab_w_out, dn_w_in, dn_conv_w, dn_a_log, dn_dt_bias, dn_o_norm, dn_w_out,
           final_norm):
    batch, seq, _ = x.shape
    depth = ffn1_norm.shape[0]
    outs = []
    for bi in range(batch):
        xs = x[bi]
        ms = mem[bi]
        for layer in range(depth):
            j = layer // 2
            last = layer == depth - 1
            xs = _ffn(xs, _row(ffn1_norm[layer]), ffn1_w_in, ffn1_w_out, _row(final_norm), layer,
                      final_norm=False)
            if layer % 2 == 0:
                q, k, v, xr, gr = _proj0(xs, _row(mix_norm[layer]), ab_w_in, j)
                attn = _attention(q, k, v)
                y = _lru(xr, gr, lru_conv_w[j].astype(F32), _row(lru_conv_b[j]),
                         _block_diag(lru_w_a[j]).astype(BF16), _row(lru_b_a[j]),
                         _block_diag(lru_w_x[j]).astype(BF16), _row(lru_b_x[j]), _row(lru_lambda[j]))
                mix_ys, mix_w = (attn, y), ab_w_out
            else:
                w = dn_w_in[j]
                wab = w[:, 4 * DN_WIDTH:]
                n_ab = wab.shape[1]
                wab_pad = jnp.pad(wab, ((0, 0), (0, LANES - n_ab))).astype(BF16)
                q, k, v, z, ab, abt = _dnprep(
                    xs, _row(mix_norm[layer]), dn_w_in, j, wab_pad, wab.T.astype(BF16),
                    dn_conv_w[j].astype(F32))
                pad = jnp.zeros((LANES - DN_HEADS,), F32)
                prow = jnp.stack([jnp.concatenate([dn_a_log[j].astype(F32), pad]),
                                  jnp.concatenate([dn_dt_bias[j].astype(F32), pad])])
                pad2 = jnp.zeros((n_ab - DN_HEADS,), F32)
                pcol = jnp.stack([jnp.concatenate([dn_a_log[j].astype(F32), pad2]),
                                  jnp.concatenate([dn_dt_bias[j].astype(F32), pad2])], axis=1)
                y = _deltanet(q, k, v, z, ab, abt, prow, pcol, _row(dn_o_norm[j]))
                mix_ys, mix_w = (y,), dn_w_out
            mk, mv = _memkv(ms, _row(xa_mem_norm[layer]), xa_wkv[layer].astype(BF16))
            xs = _xattn(xs, mix_ys, mix_w, j, _row(xa_norm[layer]), xa_wq, mk, mv, xa_wo, layer)
            xs = _ffn(xs, _row(ffn2_norm[layer]), ffn2_w_in, ffn2_w_out, _row(final_norm), layer,
                      final_norm=last)
        outs.append(xs)
    return outs[0][None] if batch == 1 else jnp.stack(outs, axis=0)
```

```python
import functools
import math

import jax
import jax.numpy as jnp
from jax import lax
from jax.experimental import pallas as pl
from jax.experimental.pallas import tpu as pltpu

F32 = jnp.float32
BF16 = jnp.bfloat16

NORM_EPS = 1e-6
L2_EPS = 1e-6
LANES = 128
SUBLANES = 8

D_MODEL = 1024
D_FF = 2816
MXU_DIM = 256
FF_CHUNKS = (6 * MXU_DIM, 5 * MXU_DIM)
assert sum(FF_CHUNKS) == D_FF
WEIGHT_STAGE_CHUNKS = 8
ATTN_HEADS = 8
ATTN_HEAD_DIM = 64
ATTN_WIDTH = 512
ATTN_PAIRS = ATTN_WIDTH // LANES
DILATIONS = (1, 4, 16)
N_BACK = 128
ATTN_BLOCK = 128
ATTN_TILE = 2048
LRU_WIDTH = 512
LRU_C = 8.0
CONV_K = 4
CONV_PAD = SUBLANES
DN_HEADS = 8
DN_HEAD_DIM = 128
DN_WIDTH = 1024
DN_CHUNK = 128
DN_INV_BLOCK = 16
DN_UNROLL = 4
DN_PREP_ROWS = 64
XA_HEADS = 4
XA_HEAD_DIM = 256
NEG_BIG = -1e30
LOG2_E = math.log2(math.e)
VMEM_LIMIT = 56 * 1024 * 1024


def _cparams(sem):
    return pltpu.CompilerParams(dimension_semantics=sem, vmem_limit_bytes=VMEM_LIMIT)


def _resident(shape, layer=None):
    if layer is None:
        nd = len(shape)
        return pl.BlockSpec(shape, lambda *_: (0,) * nd, pipeline_mode=pl.Buffered(1))
    nd = len(shape) - 1
    return pl.BlockSpec((None,) + tuple(shape[1:]), lambda *_: (layer,) + (0,) * nd,
                        pipeline_mode=pl.Buffered(1))


def _dot(a, b):
    return jnp.dot(a, b, preferred_element_type=F32)


def _dot_nt(a, b):
    return lax.dot_general(a, b, (((1,), (1,)), ((), ())), preferred_element_type=F32)


def _dot_tn(a, b):
    return lax.dot_general(a, b, (((0,), (0,)), ((), ())), preferred_element_type=F32)


def _rms(x, g):
    return x * lax.rsqrt(jnp.mean(x * x, axis=-1, keepdims=True) + NORM_EPS) * g


def _sigmoid(x):
    return 1.0 / (1.0 + jnp.exp(-x))


def _silu(x):
    return x * _sigmoid(x)


def _softplus(x):
    return jnp.maximum(x, 0.0) + jnp.log1p(jnp.exp(-jnp.abs(x)))


def _causal_conv(ext, w, tm):
    y = w[CONV_K - 1:CONV_K, :] * ext[CONV_PAD:CONV_PAD + tm]
    for shift in range(1, CONV_K):
        k = CONV_K - 1 - shift
        y = y + w[k:k + 1, :] * pltpu.roll(ext, shift, 0)[CONV_PAD:CONV_PAD + tm]
    return y


def _gelu_tanh(x):
    c = math.sqrt(2.0 / math.pi)
    return 0.5 * x * (1.0 + jnp.tanh(c * (x + 0.044715 * (x * x * x))))


def _stage_weight(src_hbm, layer, dst, stage, sem):
    rows = stage.shape[1]
    n = dst.shape[0] // rows

    def copy(c):
        return pltpu.make_async_copy(src_hbm.at[layer, pl.ds(c * rows, rows), :], stage.at[c % 2], sem.at[c % 2])

    copy(0).start()
    for c in range(n):
        if c + 1 < n:
            copy(c + 1).start()
        copy(c).wait()
        dst[c * rows:(c + 1) * rows, :] = stage[c % 2].astype(BF16)


def _staged_weight_scratch(shape):
    rows, cols = shape
    return [pltpu.VMEM((rows, cols), BF16), pltpu.VMEM((2, rows // WEIGHT_STAGE_CHUNKS, cols), F32),
            pltpu.SemaphoreType.DMA((2,))]


def _ffn_body(x_ref, g_ref, win_hbm, wout_hbm, fg_ref, o_ref, win_ref, sin, sem_in, wout_ref, sout, sem_out,
              *, layer, final_norm):
    @pl.when(pl.program_id(0) == 0)
    def _():
        _stage_weight(win_hbm, layer, win_ref, sin, sem_in)
        _stage_weight(wout_hbm, layer, wout_ref, sout, sem_out)

    x = x_ref[...]
    h = _rms(x, g_ref[...]).astype(BF16)
    acc = jnp.zeros_like(x)
    lo = 0
    for width in FF_CHUNKS:
        gate = _dot(h, win_ref[:, lo:lo + width])
        up = _dot(h, win_ref[:, D_FF + lo:D_FF + lo + width])
        a = (_silu(gate) * up).astype(BF16)
        acc = acc + _dot(a, wout_ref[lo:lo + width, :])
        lo += width
    y = x + 0.5 * acc
    if final_norm:
        y = _rms(y, fg_ref[...])
    o_ref[...] = y


def _ffn(x, g, w_in, w_out, fg, layer, *, final_norm, tm=512):
    s = x.shape[0]
    row = pl.BlockSpec((tm, D_MODEL), lambda i: (i, 0))
    hbm = pl.BlockSpec(memory_space=pl.ANY)
    return pl.pallas_call(
        functools.partial(_ffn_body, layer=layer, final_norm=final_norm),
        out_shape=jax.ShapeDtypeStruct((s, D_MODEL), F32),
        grid=(s // tm,),
        in_specs=[row, _resident((1, D_MODEL)), hbm, hbm, _resident((1, D_MODEL))],
        out_specs=row,
        scratch_shapes=_staged_weight_scratch(w_in.shape[1:]) + _staged_weight_scratch(w_out.shape[1:]),
        compiler_params=_cparams(("arbitrary",)),
        name="ffn_final" if final_norm else "ffn",
    )(x, g, w_in, w_out, fg)


def _proj0_body(x_ref, g_ref, w_hbm, *refs, tm, layer):
    nd = len(DILATIONS)
    qkv_refs = (refs[0:nd], refs[nd:2 * nd], refs[2 * nd:3 * nd])
    xr_ref, gr_ref = refs[3 * nd], refs[3 * nd + 1]
    w_ref, stage, sem = refs[3 * nd + 2:]

    @pl.when(pl.program_id(0) == 0)
    def _():
        _stage_weight(w_hbm, layer, w_ref, stage, sem)

    h = _rms(x_ref[...], g_ref[...]).astype(BF16)
    p = _dot(h, w_ref[...])
    xr_ref[...] = p[:, 3 * ATTN_WIDTH:3 * ATTN_WIDTH + LRU_WIDTH]
    gr_ref[...] = p[:, 3 * ATTN_WIDTH + LRU_WIDTH:]
    scale = ATTN_HEAD_DIM ** -0.5 * LOG2_E
    qkv = jnp.concatenate([p[:, :ATTN_WIDTH] * scale, p[:, ATTN_WIDTH:3 * ATTN_WIDTH]], axis=1).astype(BF16)

    ri = lax.broadcasted_iota(jnp.int32, (MXU_DIM, MXU_DIM), 0)
    ci = lax.broadcasted_iota(jnp.int32, (MXU_DIM, MXU_DIM), 1)
    for gi, d in enumerate(DILATIONS):
        per = MXU_DIM // d
        perm = (ci == (ri % per) * d + ri // per).astype(BF16) if d > 1 else None
        for c in range(tm // MXU_DIM):
            chunk = qkv[c * MXU_DIM:(c + 1) * MXU_DIM]
            if perm is not None:
                chunk = _dot(perm, chunk).astype(BF16)
            for r in range(d):
                piece = chunk[r * per:(r + 1) * per]
                for a in range(3):
                    for j in range(ATTN_PAIRS):
                        col = a * ATTN_WIDTH + j * LANES
                        qkv_refs[a][gi][j, c * per:(c + 1) * per, r * LANES:(r + 1) * LANES] = (
                            piece[:, col:col + LANES])


def _proj0(x, g, w, layer, *, tm=1024):
    s = x.shape[0]
    nd = len(DILATIONS)
    row = pl.BlockSpec((tm, D_MODEL), lambda i: (i, 0))
    half = pl.BlockSpec((tm, LRU_WIDTH), lambda i: (i, 0))
    view_shapes = [jax.ShapeDtypeStruct((ATTN_PAIRS, s // d, d * LANES), BF16) for d in DILATIONS]
    view_specs = [pl.BlockSpec((ATTN_PAIRS, tm // d, d * LANES), lambda i: (0, i, 0)) for d in DILATIONS]
    half_shape = jax.ShapeDtypeStruct((s, LRU_WIDTH), F32)
    outs = pl.pallas_call(
        functools.partial(_proj0_body, tm=tm, layer=layer),
        out_shape=tuple(view_shapes * 3) + (half_shape, half_shape),
        grid=(s // tm,),
        in_specs=[row, _resident((1, D_MODEL)), pl.BlockSpec(memory_space=pl.ANY)],
        out_specs=tuple(view_specs * 3) + (half, half),
        scratch_shapes=_staged_weight_scratch(w.shape[1:]),
        compiler_params=_cparams(("arbitrary",)),
        name="proj0",
    )(x, g, w)
    return outs[0:nd], outs[nd:2 * nd], outs[2 * nd:3 * nd], outs[3 * nd], outs[3 * nd + 1]


def _attn_body(*refs):
    nd = len(DILATIONS)
    q_refs, k_refs, v_refs = refs[0:nd], refs[nd:2 * nd], refs[2 * nd:3 * nd]
    o_ref = refs[3 * nd]
    khist, vhist = refs[3 * nd + 1:4 * nd + 1], refs[4 * nd + 1:5 * nd + 1]
    og_ref, lg_ref = refs[5 * nd + 1], refs[5 * nd + 2]
    t = pl.program_id(1)

    @pl.when(t == 0)
    def _():
        for h_ref in khist + vhist:
            h_ref[...] = jnp.zeros(h_ref.shape, BF16)

    nk = N_BACK + ATTN_BLOCK
    qi = lax.broadcasted_iota(jnp.int32, (ATTN_BLOCK, nk), 0)
    kj = lax.broadcasted_iota(jnp.int32, (ATTN_BLOCK, nk), 1)
    dist = qi + N_BACK - kj
    band = (dist >= 0) & (dist <= N_BACK)
    bias = jnp.where(band, 0.0, NEG_BIG)
    bias_first = jnp.where(band & (kj >= jnp.where(t == 0, N_BACK, 0)), 0.0, NEG_BIG)
    lane = lax.broadcasted_iota(jnp.int32, (1, LANES), 1)
    head_mask = [(lane < ATTN_HEAD_DIM).astype(BF16), (lane >= ATTN_HEAD_DIM).astype(BF16)]
    lane_lo = lax.broadcasted_iota(jnp.int32, (ATTN_BLOCK, LANES), 1) < ATTN_HEAD_DIM

    for gi, d in enumerate(DILATIONS):
        q_ref, k_ref, v_ref = q_refs[gi], k_refs[gi], v_refs[gi]
        for r in range(d):
            lanes = slice(r * LANES, (r + 1) * LANES)
            for sub in range(ATTN_TILE // d // ATTN_BLOCK):
                lo = ATTN_BLOCK * sub
                q = q_ref[0, lo:lo + ATTN_BLOCK, lanes]
                if sub == 0:
                    kk = jnp.concatenate([khist[gi][:, lanes], k_ref[0, 0:ATTN_BLOCK, lanes]], axis=0)
                    vv = jnp.concatenate([vhist[gi][:, lanes], v_ref[0, 0:ATTN_BLOCK, lanes]], axis=0)
                else:
                    kk = k_ref[0, lo - N_BACK:lo + ATTN_BLOCK, lanes]
                    vv = v_ref[0, lo - N_BACK:lo + ATTN_BLOCK, lanes]
                o_pair = None
                l_pair = None
                for hp in range(2):
                    s = _dot_nt(q * head_mask[hp], kk) + (bias_first if sub == 0 else bias)
                    m = jnp.max(s, axis=-1, keepdims=True)
                    p = jnp.exp2(s - m)
                    den = jnp.sum(p, axis=-1, keepdims=True)
                    o = _dot(p.astype(BF16), vv) / den
                    lse = jnp.broadcast_to(m + jnp.log2(den), (ATTN_BLOCK, LANES))
                    o_pair = o if hp == 0 else jnp.where(lane_lo, o_pair, o)
                    l_pair = lse if hp == 0 else jnp.where(lane_lo, l_pair, lse)
                tok_rows = pl.ds(r + d * lo, ATTN_BLOCK, stride=d)
                og_ref[gi, tok_rows, :] = o_pair
                lg_ref[gi, tok_rows, :] = l_pair
        rows = ATTN_TILE // d
        khist[gi][...] = k_ref[0, rows - N_BACK:rows, :]
        vhist[gi][...] = v_ref[0, rows - N_BACK:rows, :]

    l0, l1, l2 = lg_ref[0], lg_ref[1], lg_ref[2]
    lm = jnp.maximum(jnp.maximum(l0, l1), l2)
    w0, w1, w2 = jnp.exp2(l0 - lm), jnp.exp2(l1 - lm), jnp.exp2(l2 - lm)
    out = (w0 * og_ref[0] + w1 * og_ref[1] + w2 * og_ref[2]) / (w0 + w1 + w2)
    o_ref[0] = out.astype(BF16)


def _attention(q_views, k_views, v_views):
    s = q_views[0].shape[1]
    views = list(q_views) + list(k_views) + list(v_views)
    specs = [pl.BlockSpec((1, ATTN_TILE // d, d * LANES), lambda p, t: (p, t, 0)) for d in DILATIONS] * 3
    hist = [pltpu.VMEM((N_BACK, d * LANES), BF16) for d in DILATIONS]
    return pl.pallas_call(
        _attn_body,
        out_shape=jax.ShapeDtypeStruct((ATTN_PAIRS, s, LANES), BF16),
        grid=(ATTN_PAIRS, s // ATTN_TILE),
        in_specs=specs,
        out_specs=pl.BlockSpec((1, ATTN_TILE, LANES), lambda p, t: (p, t, 0)),
        scratch_shapes=hist + hist + [pltpu.VMEM((len(DILATIONS), ATTN_TILE, LANES), F32),
                                      pltpu.VMEM((len(DILATIONS), ATTN_TILE, LANES), F32)],
        compiler_params=_cparams(("arbitrary", "arbitrary")),
        name="dilated_attn",
    )(*views)


def _scan_step(a, b, k, pos):
    ok = pos >= k
    a_sh = pltpu.roll(a, k, 0)
    b_sh = pltpu.roll(b, k, 0)
    return jnp.where(ok, a * a_sh, a), jnp.where(ok, a * b_sh + b, b)


def _lru_body(xr_ref, gr_ref, cw_ref, cb_ref, wa_ref, ba_ref, wx_ref, bx_ref, lam_ref, y_ref,
              ext_ref, carry_ref, a_s, b_s, cin_s, *, tm):
    i = pl.program_id(0)

    @pl.when(i == 0)
    def _():
        ext_ref[0:CONV_PAD] = jnp.zeros((CONV_PAD, LRU_WIDTH), F32)
        carry_ref[...] = jnp.zeros((SUBLANES, LRU_WIDTH), F32)

    @pl.when(i > 0)
    def _():
        ext_ref[0:CONV_PAD] = ext_ref[tm:tm + CONV_PAD]

    ext_ref[CONV_PAD:CONV_PAD + tm] = xr_ref[...]
    xc = _causal_conv(ext_ref[...], cw_ref[...], tm) + cb_ref[...]
    xcb = xc.astype(BF16)
    r = _sigmoid(_dot(xcb, wa_ref[...]) + ba_ref[...])
    ig = _sigmoid(_dot(xcb, wx_ref[...]) + bx_ref[...])
    log_a = -LRU_C * r * _softplus(-lam_ref[...])
    a = jnp.exp(log_a)
    b = jnp.sqrt(-jnp.tanh(log_a) * (a * a + 1.0)) * ig * xc

    pos = lax.broadcasted_iota(jnp.int32, (tm, LRU_WIDTH), 0) & (SUBLANES - 1)
    for k in (1, 2, 4):
        a, b = _scan_step(a, b, k, pos)
    lane_groups = [slice(g * LANES, (g + 1) * LANES) for g in range(LRU_WIDTH // LANES)]
    for g, cols in enumerate(lane_groups):
        a_s[g] = a[:, cols]
        b_s[g] = b[:, cols]
    ng = tm // SUBLANES
    ends = pl.ds(SUBLANES - 1, ng, stride=SUBLANES)
    ae = jnp.concatenate([a_s[g, ends, :] for g in range(len(lane_groups))], axis=1)
    be = jnp.concatenate([b_s[g, ends, :] for g in range(len(lane_groups))], axis=1)
    gpos = lax.broadcasted_iota(jnp.int32, (ng, LRU_WIDTH), 0)
    k = 1
    while k < ng:
        ae, be = _scan_step(ae, be, k, gpos)
        k *= 2
    carry = carry_ref[0:1, :]
    h_end = ae * carry + be
    h_in = jnp.where(gpos >= 1, pltpu.roll(h_end, 1, 0), carry)
    for g, cols in enumerate(lane_groups):
        for j in range(SUBLANES):
            cin_s[g, pl.ds(j, ng, stride=SUBLANES), :] = h_in[:, cols]
    carry_ref[...] = jnp.broadcast_to(h_end[ng - 1:ng, :], (SUBLANES, LRU_WIDTH))
    h = jnp.concatenate([a_s[g] * cin_s[g] + b_s[g] for g in range(len(lane_groups))], axis=1)
    y_ref[...] = (h * _gelu_tanh(gr_ref[...])).astype(BF16)


def _lru(xr, gr, cw, cb, wa, ba, wx, bx, lam, *, tm=512):
    s = xr.shape[0]
    row = pl.BlockSpec((tm, LRU_WIDTH), lambda i: (i, 0))
    vec = _resident((1, LRU_WIDTH))
    sq = _resident((LRU_WIDTH, LRU_WIDTH))
    return pl.pallas_call(
        functools.partial(_lru_body, tm=tm),
        out_shape=jax.ShapeDtypeStruct((s, LRU_WIDTH), BF16),
        grid=(s // tm,),
        in_specs=[row, row, _resident((CONV_K, LRU_WIDTH)), vec, sq, vec, sq, vec, vec],
        out_specs=row,
        scratch_shapes=[pltpu.VMEM((tm + CONV_PAD, LRU_WIDTH), F32),
                        pltpu.VMEM((SUBLANES, LRU_WIDTH), F32),
                        pltpu.VMEM((LRU_WIDTH // LANES, tm, LANES), F32),
                        pltpu.VMEM((LRU_WIDTH // LANES, tm, LANES), F32),
                        pltpu.VMEM((LRU_WIDTH // LANES, tm, LANES), F32)],
        compiler_params=_cparams(("arbitrary",)),
        name="rglru",
    )(xr, gr, cw, cb, wa, ba, wx, bx, lam)


def _memkv_body(m_ref, g_ref, w_ref, k_ref, v_ref):
    h = _rms(m_ref[...], g_ref[...]).astype(BF16)
    kv = _dot(h, w_ref[...])
    k_ref[...] = kv[:, :D_MODEL].astype(BF16)
    v_ref[...] = kv[:, D_MODEL:].astype(BF16)


def _memkv(mem, g, wkv):
    n = mem.shape[0]
    shp = jax.ShapeDtypeStruct((n, D_MODEL), BF16)
    return pl.pallas_call(
        _memkv_body, out_shape=(shp, shp),
        compiler_params=pltpu.CompilerParams(vmem_limit_bytes=VMEM_LIMIT),
        name="mem_kv",
    )(mem, g, wkv)


def _xattn_body(x_ref, *refs, n_mix, mix_layer, layer):
    y_refs, wmix_hbm = refs[:n_mix], refs[n_mix]
    g_ref, wq_hbm, k_ref, v_ref, wo_hbm, o_ref, wmix_ref, wq_ref, wo_ref, stage, sem = refs[n_mix + 1:]

    @pl.when(pl.program_id(0) == 0)
    def _():
        _stage_weight(wmix_hbm, mix_layer, wmix_ref, stage, sem)
        _stage_weight(wq_hbm, layer, wq_ref, stage, sem)
        _stage_weight(wo_hbm, layer, wo_ref, stage, sem)

    parts = []
    for r in y_refs:
        if len(r.shape) == 3:
            parts.extend(r[j] for j in range(r.shape[0]))
        else:
            parts.append(r[...])
    y = parts[0] if len(parts) == 1 else jnp.concatenate(parts, axis=1)
    x = x_ref[...] + _dot(y, wmix_ref[...])
    h = _rms(x, g_ref[...]).astype(BF16)
    q = (_dot(h, wq_ref[...]) * (XA_HEAD_DIM ** -0.5)).astype(BF16)
    outs = []
    for hh in range(XA_HEADS):
        cols = slice(hh * XA_HEAD_DIM, (hh + 1) * XA_HEAD_DIM)
        s = _dot_nt(q[:, cols], k_ref[:, cols])
        p = jnp.exp(s - jnp.max(s, axis=-1, keepdims=True))
        p = (p / jnp.sum(p, axis=-1, keepdims=True)).astype(BF16)
        outs.append(_dot(p, v_ref[:, cols]).astype(BF16))
    o_ref[...] = x + _dot(jnp.concatenate(outs, axis=1), wo_ref[...])


def _xattn(x, ys, w_mix, mix_layer, g, wq, k, v, wo, layer, *, tm=1024):
    s = x.shape[0]
    hbm = pl.BlockSpec(memory_space=pl.ANY)
    assert w_mix.shape[1:] == wq.shape[1:] == wo.shape[1:] == (D_MODEL, D_MODEL)
    wbuf = pltpu.VMEM((D_MODEL, D_MODEL), BF16)
    row = pl.BlockSpec((tm, D_MODEL), lambda i: (i, 0))
    y_specs = []
    for y in ys:
        if y.ndim == 3:
            y_specs.append(pl.BlockSpec((y.shape[0], tm, y.shape[2]), lambda i: (0, i, 0)))
        else:
            y_specs.append(pl.BlockSpec((tm, y.shape[1]), lambda i: (i, 0)))
    return pl.pallas_call(
        functools.partial(_xattn_body, n_mix=len(ys), mix_layer=mix_layer, layer=layer),
        out_shape=jax.ShapeDtypeStruct((s, D_MODEL), F32),
        grid=(s // tm,),
        in_specs=[row] + y_specs + [hbm, _resident((1, D_MODEL)), hbm, _resident(k.shape),
                                    _resident(v.shape), hbm],
        out_specs=row,
        scratch_shapes=[wbuf, wbuf, wbuf,
                        pltpu.VMEM((2, D_MODEL // WEIGHT_STAGE_CHUNKS, D_MODEL), F32),
                        pltpu.SemaphoreType.DMA((2,))],
        compiler_params=_cparams(("arbitrary",)),
        name="mix_out_cross_attn",
    )(x, *ys, w_mix, g, wq, k, v, wo)


def _dnprep_body(x_ref, g_ref, w_ref, wab_ref, wabt_ref, cw_ref,
                 q_ref, k_ref, v_ref, z_ref, ab_ref, abt_ref, ext_ref, *, tm):
    i = pl.program_id(0)
    width = 3 * DN_WIDTH

    @pl.when(i == 0)
    def _():
        ext_ref[0:CONV_PAD] = jnp.zeros((CONV_PAD, width), F32)

    h = _rms(x_ref[...], g_ref[...]).astype(BF16)
    outs = (q_ref, k_ref, v_ref)
    qkv_blocks = width // MXU_DIM
    z_blocks = DN_WIDTH // MXU_DIM
    for b in range(qkv_blocks):
        if b % (qkv_blocks // z_blocks) == 0:
            zc = (b // (qkv_blocks // z_blocks)) * MXU_DIM
            z_ref[:, zc:zc + MXU_DIM] = _dot(h, w_ref[:, width + zc:width + zc + MXU_DIM])
        c0 = b * MXU_DIM
        cols = slice(c0, c0 + MXU_DIM)
        ext_ref[CONV_PAD:CONV_PAD + tm, cols] = _dot(h, w_ref[:, cols])
        cw = cw_ref[:, cols]
        for r0 in range(0, tm, DN_PREP_ROWS):
            act = _silu(_causal_conv(ext_ref[r0:r0 + DN_PREP_ROWS + CONV_PAD, cols], cw, DN_PREP_ROWS))
            for c1 in range(c0, c0 + MXU_DIM, DN_HEAD_DIM):
                part, hh = divmod(c1 // DN_HEAD_DIM, DN_HEADS)
                u = act[:, c1 - c0:c1 - c0 + DN_HEAD_DIM]
                if part < 2:
                    u = u * lax.rsqrt(jnp.sum(u * u, axis=-1, keepdims=True) + L2_EPS)
                if part == 0:
                    u = u * (DN_HEAD_DIM ** -0.5)
                outs[part][hh, r0:r0 + DN_PREP_ROWS, :] = u.astype(BF16)
    ext_ref[0:CONV_PAD] = ext_ref[tm:tm + CONV_PAD]

    ab_ref[...] = _dot(h, wab_ref[...])
    abt = _dot_nt(wabt_ref[...], h)
    for j in range(tm // DN_CHUNK):
        abt_ref[j] = abt[:, j * DN_CHUNK:(j + 1) * DN_CHUNK]


def _dnprep(x, g, w, wab, wabt, cw, *, tm=256):
    s = x.shape[0]
    row = pl.BlockSpec((tm, D_MODEL), lambda i: (i, 0))
    head = pl.BlockSpec((DN_HEADS, tm, DN_HEAD_DIM), lambda i: (0, i, 0))
    head_shape = jax.ShapeDtypeStruct((DN_HEADS, s, DN_HEAD_DIM), BF16)
    nab = wabt.shape[0]
    return pl.pallas_call(
        functools.partial(_dnprep_body, tm=tm),
        out_shape=(head_shape, head_shape, head_shape,
                   jax.ShapeDtypeStruct((s, DN_WIDTH), F32),
                   jax.ShapeDtypeStruct((s, LANES), F32),
                   jax.ShapeDtypeStruct((s // DN_CHUNK, nab, DN_CHUNK), F32)),
        grid=(s // tm,),
        in_specs=[row, _resident((1, D_MODEL)), _resident(w.shape),
                  _resident(wab.shape), _resident(wabt.shape), _resident(cw.shape)],
        out_specs=(head, head, head, row,
                   pl.BlockSpec((tm, LANES), lambda i: (i, 0)),
                   pl.BlockSpec((tm // DN_CHUNK, nab, DN_CHUNK), lambda i: (i, 0, 0))),
        scratch_shapes=[pltpu.VMEM((tm + CONV_PAD, 3 * DN_WIDTH), F32)],
        compiler_params=_cparams(("arbitrary",)),
        name="dn_prep",
    )(x, g, w, wab, wabt, cw)


def _dn_body(q_ref, k_ref, v_ref, z_ref, ab_ref, abt_ref, prow_ref, pcol_ref, on_ref, y_ref, state, *, ts):
    c = DN_CHUNK

    @pl.when(pl.program_id(0) == 0)
    def _():
        state[...] = jnp.zeros((DN_HEADS, DN_HEAD_DIM, DN_HEAD_DIM), F32)

    ri = lax.broadcasted_iota(jnp.int32, (c, c), 0)
    ci = lax.broadcasted_iota(jnp.int32, (c, c), 1)
    lower = ri >= ci
    strict = ri > ci
    tri_l = lower.astype(F32)
    tri_u = (ri <= ci).astype(F32)
    eye = (ri == ci).astype(F32)
    diag_blk = (ri // DN_INV_BLOCK) == (ci // DN_INV_BLOCK)
    off_blks = []
    bsz = DN_INV_BLOCK
    while bsz < c:
        off_blks.append(((ri // (2 * bsz)) == (ci // (2 * bsz))) & ((ri // bsz) != (ci // bsz)))
        bsz *= 2
    hi = lax.Precision.HIGHEST
    a_log_row, dtb_row = prow_ref[0:1, :], prow_ref[1:2, :]
    a_log_col, dtb_col = pcol_ref[:, 0:1], pcol_ref[:, 1:2]
    onorm = on_ref[...]

    heads = range(DN_HEADS)

    def step(it, _):
        rows, gcol, beta, decay, egc, kk, qq, kf = [], [], [], [], [], [], [], []
        for u in range(DN_UNROLL):
            ci_ = it * DN_UNROLL + u
            r = pl.ds(pl.multiple_of(ci_ * c, c), c)
            ab = ab_ref[r, :]
            abt = abt_ref[ci_]
            g_col = -jnp.exp(a_log_row) * _softplus(ab + dtb_row)
            gc_col = jnp.dot(tri_l, g_col, precision=hi, preferred_element_type=F32)
            beta_all = _sigmoid(ab)
            g_row = -jnp.exp(a_log_col) * _softplus(abt + dtb_col)
            gc_row = jnp.dot(g_row, tri_u, precision=hi, preferred_element_type=F32)
            for h in heads:
                rows.append(r)
                gcol.append(gc_col[:, h:h + 1])
                beta.append(beta_all[:, DN_HEADS + h:DN_HEADS + h + 1])
                decay.append(jnp.where(lower, jnp.exp(jnp.minimum(gcol[-1] - gc_row[h:h + 1, :], 0.0)), 0.0))
                egc.append(jnp.exp(gcol[-1]))
                kk.append(k_ref[h, r, :])
                qq.append(q_ref[h, r, :])
                kf.append(kk[-1].astype(F32))
        items = range(DN_UNROLL * DN_HEADS)
        kb = [kf[i] * beta[i] for i in items]
        m = [jnp.where(strict, _dot_nt(kb[i].astype(BF16), kk[i]) * decay[i], 0.0) * -1.0 for i in items]
        qk = [jnp.where(lower, _dot_nt(qq[i], kk[i]) * decay[i], 0.0).astype(BF16) for i in items]
        pw = [jnp.where(diag_blk, m[i], 0.0) for i in items]
        t_inv = [eye + pw[i] for i in items]
        for _ in range(int(math.log2(DN_INV_BLOCK)) - 1):
            pwb = [pw[i].astype(BF16) for i in items]
            pw = [_dot(pwb[i], pwb[i]) for i in items]
            t_inv = [t_inv[i] + _dot(t_inv[i].astype(BF16), pw[i].astype(BF16)) for i in items]
        for off_blk in off_blks:
            tb = [t_inv[i].astype(BF16) for i in items]
            x = [_dot(jnp.where(off_blk, m[i], 0.0).astype(BF16), tb[i]).astype(BF16) for i in items]
            t_inv = [t_inv[i] + _dot(tb[i], x[i]) for i in items]
        rhs = [jnp.concatenate([v_ref[i % DN_HEADS, rows[i], :].astype(F32) * beta[i], kb[i] * egc[i]],
                               axis=1).astype(BF16) for i in items]
        sol = [_dot(t_inv[i].astype(BF16), rhs[i]) for i in items]
        qg = [(qq[i].astype(F32) * egc[i]).astype(BF16) for i in items]
        kd, eg_last = [], []
        for i in items:
            glast = gcol[i][c - 1:c, :]
            kd.append((kf[i] * jnp.exp(glast - gcol[i])).astype(BF16))
            eg_last.append(jnp.exp(glast))
        for u in range(DN_UNROLL):
            idx = [u * DN_HEADS + h for h in heads]
            st = [state[h] for h in heads]
            stb = [st[h].astype(BF16) for h in heads]
            v_new = [sol[i][:, :DN_HEAD_DIM] - _dot(sol[i][:, DN_HEAD_DIM:].astype(BF16), stb[h])
                     for h, i in enumerate(idx)]
            vnb = [v.astype(BF16) for v in v_new]
            for h, i in enumerate(idx):
                state[h] = st[h] * eg_last[i] + _dot_tn(kd[i], vnb[h])
            o = [_dot(qg[i], stb[h]) + _dot(qk[i], vnb[h]) for h, i in enumerate(idx)]
            for h, i in enumerate(idx):
                cols = slice(h * DN_HEAD_DIM, (h + 1) * DN_HEAD_DIM)
                on = o[h] * lax.rsqrt(jnp.mean(o[h] * o[h], axis=-1, keepdims=True) + NORM_EPS) * onorm
                y_ref[rows[i], cols] = (on * _silu(z_ref[rows[i], cols])).astype(BF16)
        return 0

    lax.fori_loop(0, ts // (c * DN_UNROLL), step, 0)


def _deltanet(q, k, v, z, ab, abt, prow, pcol, onorm, *, ts=1024):
    s = z.shape[0]
    head = pl.BlockSpec((DN_HEADS, ts, DN_HEAD_DIM), lambda i: (0, i, 0))
    nab = abt.shape[1]
    return pl.pallas_call(
        functools.partial(_dn_body, ts=ts),
        out_shape=jax.ShapeDtypeStruct((s, DN_WIDTH), BF16),
        grid=(s // ts,),
        in_specs=[head, head, head,
                  pl.BlockSpec((ts, DN_WIDTH), lambda i: (i, 0)),
                  pl.BlockSpec((ts, LANES), lambda i: (i, 0)),
                  pl.BlockSpec((ts // DN_CHUNK, nab, DN_CHUNK), lambda i: (i, 0, 0)),
                  _resident(prow.shape), _resident(pcol.shape), _resident(onorm.shape)],
        out_specs=pl.BlockSpec((ts, DN_WIDTH), lambda i: (i, 0)),
        scratch_shapes=[pltpu.VMEM((DN_HEADS, DN_HEAD_DIM, DN_HEAD_DIM), F32)],
        compiler_params=_cparams(("arbitrary",)),
        name="gated_deltanet",
    )(q, k, v, z, ab, abt, prow, pcol, onorm)


def _row(v):
    return v.reshape(1, -1).astype(F32)


def _block_diag(w):
    n, j, k = w.shape
    eye = jnp.eye(n, dtype=w.dtype)
    return (w[:, :, None, :] * eye[:, None, :, None]).reshape(n * j, n * k)


def kernel(x, mem, ffn1_norm, ffn1_w_in, ffn1_w_out, mix_norm, xa_norm, xa_mem_norm, xa_wq, xa_wkv, xa_wo,
           ffn2_norm, ffn2_w_in, ffn2_w_out, ab_w_in, lru_conv_w, lru_conv_b, lru_w_a, lru_b_a, lru_w_x,
           lru_b_x, lru_lambda, ab_w_out, dn_w_in, dn_conv_w, dn_a_log, dn_dt_bias, dn_o_norm, dn_w_out,
           final_norm):
    batch, seq, _ = x.shape
    depth = ffn1_norm.shape[0]
    outs = []
    for bi in range(batch):
        xs = x[bi]
        ms = mem[bi]
        for layer in range(depth):
            j = layer // 2
            last = layer == depth - 1
            xs = _ffn(xs, _row(ffn1_norm[layer]), ffn1_w_in, ffn1_w_out, _row(final_norm), layer,
                      final_norm=False)
            if layer % 2 == 0:
                q, k, v, xr, gr = _proj0(xs, _row(mix_norm[layer]), ab_w_in, j)
                attn = _attention(q, k, v)
                y = _lru(xr, gr, lru_conv_w[j].astype(F32), _row(lru_conv_b[j]),
                         _block_diag(lru_w_a[j]).astype(BF16), _row(lru_b_a[j]),
                         _block_diag(lru_w_x[j]).astype(BF16), _row(lru_b_x[j]), _row(lru_lambda[j]))
                mix_ys, mix_w = (attn, y), ab_w_out
            else:
                w = dn_w_in[j]
                wab = w[:, 4 * DN_WIDTH:]
                n_ab = wab.shape[1]
                wab_pad = jnp.pad(wab, ((0, 0), (0, LANES - n_ab))).astype(BF16)
                q, k, v, z, ab, abt = _dnprep(
                    xs, _row(mix_norm[layer]), w[:, :4 * DN_WIDTH].astype(BF16), wab_pad, wab.T.astype(BF16),
                    dn_conv_w[j].astype(F32))
                pad = jnp.zeros((LANES - DN_HEADS,), F32)
                prow = jnp.stack([jnp.concatenate([dn_a_log[j].astype(F32), pad]),
                                  jnp.concatenate([dn_dt_bias[j].astype(F32), pad])])
                pad2 = jnp.zeros((n_ab - DN_HEADS,), F32)
                pcol = jnp.stack([jnp.concatenate([dn_a_log[j].astype(F32), pad2]),
                                  jnp.concatenate([dn_dt_bias[j].astype(F32), pad2])], axis=1)
                y = _deltanet(q, k, v, z, ab, abt, prow, pcol, _row(dn_o_norm[j]))
                mix_ys, mix_w = (y,), dn_w_out
            mk, mv = _memkv(ms, _row(xa_mem_norm[layer]), xa_wkv[layer].astype(BF16))
            xs = _xattn(xs, mix_ys, mix_w, j, _row(xa_norm[layer]), xa_wq, mk, mv, xa_wo, layer)
            xs = _ffn(xs, _row(ffn2_norm[layer]), ffn2_w_in, ffn2_w_out, _row(final_norm), layer,
                      final_norm=last)
        outs.append(xs)
    return outs[0][None] if batch == 1 else jnp.stack(outs, axis=0)
```

```python
import functools
import math

import jax
import jax.numpy as jnp
from jax import lax
from jax.experimental import pallas as pl
from jax.experimental.pallas import tpu as pltpu

F32 = jnp.float32
BF16 = jnp.bfloat16

NORM_EPS = 1e-6
L2_EPS = 1e-6
LANES = 128
SUBLANES = 8

D_MODEL = 1024
D_FF = 2816
MXU_DIM = 256
FF_CHUNKS = (6 * MXU_DIM, 5 * MXU_DIM)
assert sum(FF_CHUNKS) == D_FF
WEIGHT_STAGE_CHUNKS = 8
ATTN_HEADS = 8
ATTN_HEAD_DIM = 64
ATTN_WIDTH = 512
ATTN_PAIRS = ATTN_WIDTH // LANES
DILATIONS = (1, 4, 16)
N_BACK = 128
ATTN_BLOCK = 128
ATTN_TILE = 2048
LRU_WIDTH = 512
LRU_C = 8.0
CONV_K = 4
CONV_PAD = SUBLANES
DN_HEADS = 8
DN_HEAD_DIM = 128
DN_WIDTH = 1024
DN_CHUNK = 128
DN_INV_BLOCK = 16
DN_UNROLL = 4
DN_PREP_ROWS = 64
XA_HEADS = 4
XA_HEAD_DIM = 256
NEG_BIG = -1e30
LOG2_E = math.log2(math.e)
VMEM_LIMIT = 56 * 1024 * 1024


def _cparams(sem):
    return pltpu.CompilerParams(dimension_semantics=sem, vmem_limit_bytes=VMEM_LIMIT)


def _resident(shape, layer=None):
    if layer is None:
        nd = len(shape)
        return pl.BlockSpec(shape, lambda *_: (0,) * nd, pipeline_mode=pl.Buffered(1))
    nd = len(shape) - 1
    return pl.BlockSpec((None,) + tuple(shape[1:]), lambda *_: (layer,) + (0,) * nd,
                        pipeline_mode=pl.Buffered(1))


def _dot(a, b):
    return jnp.dot(a, b, preferred_element_type=F32)


def _dot_nt(a, b):
    return lax.dot_general(a, b, (((1,), (1,)), ((), ())), preferred_element_type=F32)


def _dot_tn(a, b):
    return lax.dot_general(a, b, (((0,), (0,)), ((), ())), preferred_element_type=F32)


def _rms(x, g):
    return x * lax.rsqrt(jnp.mean(x * x, axis=-1, keepdims=True) + NORM_EPS) * g


def _sigmoid(x):
    return 1.0 / (1.0 + jnp.exp(-x))


def _silu(x):
    return x * _sigmoid(x)


def _softplus(x):
    return jnp.maximum(x, 0.0) + jnp.log1p(jnp.exp(-jnp.abs(x)))


def _causal_conv(ext, w, tm):
    y = w[CONV_K - 1:CONV_K, :] * ext[CONV_PAD:CONV_PAD + tm]
    for shift in range(1, CONV_K):
        k = CONV_K - 1 - shift
        y = y + w[k:k + 1, :] * pltpu.roll(ext, shift, 0)[CONV_PAD:CONV_PAD + tm]
    return y


def _gelu_tanh(x):
    c = math.sqrt(2.0 / math.pi)
    return 0.5 * x * (1.0 + jnp.tanh(c * (x + 0.044715 * (x * x * x))))


def _stage_weight(src_hbm, layer, dst, stage, sem):
    rows = stage.shape[1]
    n = dst.shape[0] // rows

    def copy(c):
        return pltpu.make_async_copy(src_hbm.at[layer, pl.ds(c * rows, rows), :], stage.at[c % 2], sem.at[c % 2])

    copy(0).start()
    for c in range(n):
        if c + 1 < n:
            copy(c + 1).start()
        copy(c).wait()
        dst[c * rows:(c + 1) * rows, :] = stage[c % 2].astype(BF16)


def _staged_weight_scratch(shape):
    rows, cols = shape
    return [pltpu.VMEM((rows, cols), BF16), pltpu.VMEM((2, rows // WEIGHT_STAGE_CHUNKS, cols), F32),
            pltpu.SemaphoreType.DMA((2,))]


def _ffn_body(x_ref, g_ref, win_hbm, wout_hbm, fg_ref, o_ref, win_ref, sin, sem_in, wout_ref, sout, sem_out,
              *, layer, final_norm):
    @pl.when(pl.program_id(0) == 0)
    def _():
        _stage_weight(win_hbm, layer, win_ref, sin, sem_in)
        _stage_weight(wout_hbm, layer, wout_ref, sout, sem_out)

    x = x_ref[...]
    h = _rms(x, g_ref[...]).astype(BF16)
    acc = jnp.zeros_like(x)
    lo = 0
    for width in FF_CHUNKS:
        gate = _dot(h, win_ref[:, lo:lo + width])
        up = _dot(h, win_ref[:, D_FF + lo:D_FF + lo + width])
        a = (_silu(gate) * up).astype(BF16)
        acc = acc + _dot(a, wout_ref[lo:lo + width, :])
        lo += width
    y = x + 0.5 * acc
    if final_norm:
        y = _rms(y, fg_ref[...])
    o_ref[...] = y


def _ffn(x, g, w_in, w_out, fg, layer, *, final_norm, tm=512):
    s = x.shape[0]
    row = pl.BlockSpec((tm, D_MODEL), lambda i: (i, 0))
    hbm = pl.BlockSpec(memory_space=pl.ANY)
    return pl.pallas_call(
        functools.partial(_ffn_body, layer=layer, final_norm=final_norm),
        out_shape=jax.ShapeDtypeStruct((s, D_MODEL), F32),
        grid=(s // tm,),
        in_specs=[row, _resident((1, D_MODEL)), hbm, hbm, _resident((1, D_MODEL))],
        out_specs=row,
        scratch_shapes=_staged_weight_scratch(w_in.shape[1:]) + _staged_weight_scratch(w_out.shape[1:]),
        compiler_params=_cparams(("arbitrary",)),
        name="ffn_final" if final_norm else "ffn",
    )(x, g, w_in, w_out, fg)


def _proj0_body(x_ref, g_ref, w_hbm, *refs, tm, layer):
    nd = len(DILATIONS)
    qkv_refs = (refs[0:nd], refs[nd:2 * nd], refs[2 * nd:3 * nd])
    xr_ref, gr_ref = refs[3 * nd], refs[3 * nd + 1]
    w_ref, stage, sem = refs[3 * nd + 2:]

    @pl.when(pl.program_id(0) == 0)
    def _():
        _stage_weight(w_hbm, layer, w_ref, stage, sem)

    h = _rms(x_ref[...], g_ref[...]).astype(BF16)
    p = _dot(h, w_ref[...])
    xr_ref[...] = p[:, 3 * ATTN_WIDTH:3 * ATTN_WIDTH + LRU_WIDTH]
    gr_ref[...] = p[:, 3 * ATTN_WIDTH + LRU_WIDTH:]
    scale = ATTN_HEAD_DIM ** -0.5 * LOG2_E
    qkv = jnp.concatenate([p[:, :ATTN_WIDTH] * scale, p[:, ATTN_WIDTH:3 * ATTN_WIDTH]], axis=1).astype(BF16)

    ri = lax.broadcasted_iota(jnp.int32, (MXU_DIM, MXU_DIM), 0)
    ci = lax.broadcasted_iota(jnp.int32, (MXU_DIM, MXU_DIM), 1)
    for gi, d in enumerate(DILATIONS):
        per = MXU_DIM // d
        perm = (ci == (ri % per) * d + ri // per).astype(BF16) if d > 1 else None
        for c in range(tm // MXU_DIM):
            chunk = qkv[c * MXU_DIM:(c + 1) * MXU_DIM]
            if perm is not None:
                chunk = _dot(perm, chunk).astype(BF16)
            for r in range(d):
                piece = chunk[r * per:(r + 1) * per]
                for a in range(3):
                    for j in range(ATTN_PAIRS):
                        col = a * ATTN_WIDTH + j * LANES
                        qkv_refs[a][gi][j, c * per:(c + 1) * per, r * LANES:(r + 1) * LANES] = (
                            piece[:, col:col + LANES])


def _proj0(x, g, w, layer, *, tm=1024):
    s = x.shape[0]
    nd = len(DILATIONS)
    row = pl.BlockSpec((tm, D_MODEL), lambda i: (i, 0))
    half = pl.BlockSpec((tm, LRU_WIDTH), lambda i: (i, 0))
    view_shapes = [jax.ShapeDtypeStruct((ATTN_PAIRS, s // d, d * LANES), BF16) for d in DILATIONS]
    view_specs = [pl.BlockSpec((ATTN_PAIRS, tm // d, d * LANES), lambda i: (0, i, 0)) for d in DILATIONS]
    half_shape = jax.ShapeDtypeStruct((s, LRU_WIDTH), F32)
    outs = pl.pallas_call(
        functools.partial(_proj0_body, tm=tm, layer=layer),
        out_shape=tuple(view_shapes * 3) + (half_shape, half_shape),
        grid=(s // tm,),
        in_specs=[row, _resident((1, D_MODEL)), pl.BlockSpec(memory_space=pl.ANY)],
        out_specs=tuple(view_specs * 3) + (half, half),
        scratch_shapes=_staged_weight_scratch(w.shape[1:]),
        compiler_params=_cparams(("arbitrary",)),
        name="proj0",
    )(x, g, w)
    return outs[0:nd], outs[nd:2 * nd], outs[2 * nd:3 * nd], outs[3 * nd], outs[3 * nd + 1]


def _attn_body(*refs):
    nd = len(DILATIONS)
    q_refs, k_refs, v_refs = refs[0:nd], refs[nd:2 * nd], refs[2 * nd:3 * nd]
    o_ref = refs[3 * nd]
    khist, vhist = refs[3 * nd + 1:4 * nd + 1], refs[4 * nd + 1:5 * nd + 1]
    og_ref, lg_ref = refs[5 * nd + 1], refs[5 * nd + 2]
    t = pl.program_id(1)

    @pl.when(t == 0)
    def _():
        for h_ref in khist + vhist:
            h_ref[...] = jnp.zeros(h_ref.shape, BF16)

    nk = N_BACK + ATTN_BLOCK
    qi = lax.broadcasted_iota(jnp.int32, (ATTN_BLOCK, nk), 0)
    kj = lax.broadcasted_iota(jnp.int32, (ATTN_BLOCK, nk), 1)
    dist = qi + N_BACK - kj
    band = (dist >= 0) & (dist <= N_BACK)
    bias = jnp.where(band, 0.0, NEG_BIG)
    bias_first = jnp.where(band & (kj >= jnp.where(t == 0, N_BACK, 0)), 0.0, NEG_BIG)
    lane = lax.broadcasted_iota(jnp.int32, (1, LANES), 1)
    head_mask = [(lane < ATTN_HEAD_DIM).astype(BF16), (lane >= ATTN_HEAD_DIM).astype(BF16)]
    lane_lo = lax.broadcasted_iota(jnp.int32, (ATTN_BLOCK, LANES), 1) < ATTN_HEAD_DIM

    for gi, d in enumerate(DILATIONS):
        q_ref, k_ref, v_ref = q_refs[gi], k_refs[gi], v_refs[gi]
        for r in range(d):
            lanes = slice(r * LANES, (r + 1) * LANES)
            for sub in range(ATTN_TILE // d // ATTN_BLOCK):
                lo = ATTN_BLOCK * sub
                q = q_ref[0, lo:lo + ATTN_BLOCK, lanes]
                if sub == 0:
                    kk = jnp.concatenate([khist[gi][:, lanes], k_ref[0, 0:ATTN_BLOCK, lanes]], axis=0)
                    vv = jnp.concatenate([vhist[gi][:, lanes], v_ref[0, 0:ATTN_BLOCK, lanes]], axis=0)
                else:
                    kk = k_ref[0, lo - N_BACK:lo + ATTN_BLOCK, lanes]
                    vv = v_ref[0, lo - N_BACK:lo + ATTN_BLOCK, lanes]
                o_pair = None
                l_pair = None
                for hp in range(2):
                    s = _dot_nt(q * head_mask[hp], kk) + (bias_first if sub == 0 else bias)
                    m = jnp.max(s, axis=-1, keepdims=True)
                    p = jnp.exp2(s - m)
                    den = jnp.sum(p, axis=-1, keepdims=True)
                    o = _dot(p.astype(BF16), vv) / den
                    lse = jnp.broadcast_to(m + jnp.log2(den), (ATTN_BLOCK, LANES))
                    o_pair = o if hp == 0 else jnp.where(lane_lo, o_pair, o)
                    l_pair = lse if hp == 0 else jnp.where(lane_lo, l_pair, lse)
                tok_rows = pl.ds(r + d * lo, ATTN_BLOCK, stride=d)
                og_ref[gi, tok_rows, :] = o_pair
                lg_ref[gi, tok_rows, :] = l_pair
        rows = ATTN_TILE // d
        khist[gi][...] = k_ref[0, rows - N_BACK:rows, :]
        vhist[gi][...] = v_ref[0, rows - N_BACK:rows, :]

    l0, l1, l2 = lg_ref[0], lg_ref[1], lg_ref[2]
    lm = jnp.maximum(jnp.maximum(l0, l1), l2)
    w0, w1, w2 = jnp.exp2(l0 - lm), jnp.exp2(l1 - lm), jnp.exp2(l2 - lm)
    out = (w0 * og_ref[0] + w1 * og_ref[1] + w2 * og_ref[2]) / (w0 + w1 + w2)
    o_ref[0] = out.astype(BF16)


def _attention(q_views, k_views, v_views):
    s = q_views[0].shape[1]
    views = list(q_views) + list(k_views) + list(v_views)
    specs = [pl.BlockSpec((1, ATTN_TILE // d, d * LANES), lambda p, t: (p, t, 0)) for d in DILATIONS] * 3
    hist = [pltpu.VMEM((N_BACK, d * LANES), BF16) for d in DILATIONS]
    return pl.pallas_call(
        _attn_body,
        out_shape=jax.ShapeDtypeStruct((ATTN_PAIRS, s, LANES), BF16),
        grid=(ATTN_PAIRS, s // ATTN_TILE),
        in_specs=specs,
        out_specs=pl.BlockSpec((1, ATTN_TILE, LANES), lambda p, t: (p, t, 0)),
        scratch_shapes=hist + hist + [pltpu.VMEM((len(DILATIONS), ATTN_TILE, LANES), F32),
                                      pltpu.VMEM((len(DILATIONS), ATTN_TILE, LANES), F32)],
        compiler_params=_cparams(("arbitrary", "arbitrary")),
        name="dilated_attn",
    )(*views)


def _scan_step(a, b, k, pos):
    ok = pos >= k
    a_sh = pltpu.roll(a, k, 0)
    b_sh = pltpu.roll(b, k, 0)
    return jnp.where(ok, a * a_sh, a), jnp.where(ok, a * b_sh + b, b)


def _lru_body(xr_ref, gr_ref, cw_ref, cb_ref, wa_ref, ba_ref, wx_ref, bx_ref, lam_ref, y_ref,
              ext_ref, carry_ref, a_s, b_s, cin_s, *, tm):
    i = pl.program_id(0)

    @pl.when(i == 0)
    def _():
        ext_ref[0:CONV_PAD] = jnp.zeros((CONV_PAD, LRU_WIDTH), F32)
        carry_ref[...] = jnp.zeros((SUBLANES, LRU_WIDTH), F32)

    @pl.when(i > 0)
    def _():
        ext_ref[0:CONV_PAD] = ext_ref[tm:tm + CONV_PAD]

    ext_ref[CONV_PAD:CONV_PAD + tm] = xr_ref[...]
    xc = _causal_conv(ext_ref[...], cw_ref[...], tm) + cb_ref[...]
    xcb = xc.astype(BF16)
    r = _sigmoid(_dot(xcb, wa_ref[...]) + ba_ref[...])
    ig = _sigmoid(_dot(xcb, wx_ref[...]) + bx_ref[...])
    log_a = -LRU_C * r * _softplus(-lam_ref[...])
    a = jnp.exp(log_a)
    b = jnp.sqrt(-jnp.tanh(log_a) * (a * a + 1.0)) * ig * xc

    a = a.reshape(tm // SUBLANES, SUBLANES, LRU_WIDTH)
    b = b.reshape(tm // SUBLANES, SUBLANES, LRU_WIDTH)
    pos = lax.broadcasted_iota(jnp.int32, a.shape, 1)
    for k in (1, 2, 4):
        ok = pos >= k
        a_sh = pltpu.roll(a, k, 1)
        b_sh = pltpu.roll(b, k, 1)
        a, b = jnp.where(ok, a * a_sh, a), jnp.where(ok, a * b_sh + b, b)
    a = a.reshape(tm, LRU_WIDTH)
    b = b.reshape(tm, LRU_WIDTH)
    lane_groups = [slice(g * LANES, (g + 1) * LANES) for g in range(LRU_WIDTH // LANES)]
    for g, cols in enumerate(lane_groups):
        a_s[g] = a[:, cols]
        b_s[g] = b[:, cols]
    ng = tm // SUBLANES
    ends = pl.ds(SUBLANES - 1, ng, stride=SUBLANES)
    ae = jnp.concatenate([a_s[g, ends, :] for g in range(len(lane_groups))], axis=1)
    be = jnp.concatenate([b_s[g, ends, :] for g in range(len(lane_groups))], axis=1)
    gpos = lax.broadcasted_iota(jnp.int32, (ng, LRU_WIDTH), 0)
    k = 1
    while k < ng:
        ae, be = _scan_step(ae, be, k, gpos)
        k *= 2
    carry = carry_ref[0:1, :]
    h_end = ae * carry + be
    h_in = jnp.where(gpos >= 1, pltpu.roll(h_end, 1, 0), carry)
    for g, cols in enumerate(lane_groups):
        for j in range(SUBLANES):
            cin_s[g, pl.ds(j, ng, stride=SUBLANES), :] = h_in[:, cols]
    carry_ref[...] = jnp.broadcast_to(h_end[ng - 1:ng, :], (SUBLANES, LRU_WIDTH))
    h = jnp.concatenate([a_s[g] * cin_s[g] + b_s[g] for g in range(len(lane_groups))], axis=1)
    y_ref[...] = (h * _gelu_tanh(gr_ref[...])).astype(BF16)


def _lru(xr, gr, cw, cb, wa, ba, wx, bx, lam, *, tm=512):
    s = xr.shape[0]
    row = pl.BlockSpec((tm, LRU_WIDTH), lambda i: (i, 0))
    vec = _resident((1, LRU_WIDTH))
    sq = _resident((LRU_WIDTH, LRU_WIDTH))
    return pl.pallas_call(
        functools.partial(_lru_body, tm=tm),
        out_shape=jax.ShapeDtypeStruct((s, LRU_WIDTH), BF16),
        grid=(s // tm,),
        in_specs=[row, row, _resident((CONV_K, LRU_WIDTH)), vec, sq, vec, sq, vec, vec],
        out_specs=row,
        scratch_shapes=[pltpu.VMEM((tm + CONV_PAD, LRU_WIDTH), F32),
                        pltpu.VMEM((SUBLANES, LRU_WIDTH), F32),
                        pltpu.VMEM((LRU_WIDTH // LANES, tm, LANES), F32),
                        pltpu.VMEM((LRU_WIDTH // LANES, tm, LANES), F32),
                        pltpu.VMEM((LRU_WIDTH // LANES, tm, LANES), F32)],
        compiler_params=_cparams(("arbitrary",)),
        name="rglru",
    )(xr, gr, cw, cb, wa, ba, wx, bx, lam)


def _memkv_body(m_ref, g_ref, w_ref, k_ref, v_ref):
    h = _rms(m_ref[...], g_ref[...]).astype(BF16)
    kv = _dot(h, w_ref[...])
    k_ref[...] = kv[:, :D_MODEL].astype(BF16)
    v_ref[...] = kv[:, D_MODEL:].astype(BF16)


def _memkv(mem, g, wkv):
    n = mem.shape[0]
    shp = jax.ShapeDtypeStruct((n, D_MODEL), BF16)
    return pl.pallas_call(
        _memkv_body, out_shape=(shp, shp),
        compiler_params=pltpu.CompilerParams(vmem_limit_bytes=VMEM_LIMIT),
        name="mem_kv",
    )(mem, g, wkv)


def _xattn_body(x_ref, *refs, n_mix, mix_layer, layer):
    y_refs, wmix_hbm = refs[:n_mix], refs[n_mix]
    g_ref, wq_hbm, k_ref, v_ref, wo_hbm, o_ref, wmix_ref, wq_ref, wo_ref, stage, sem = refs[n_mix + 1:]

    @pl.when(pl.program_id(0) == 0)
    def _():
        _stage_weight(wmix_hbm, mix_layer, wmix_ref, stage, sem)
        _stage_weight(wq_hbm, layer, wq_ref, stage, sem)
        _stage_weight(wo_hbm, layer, wo_ref, stage, sem)

    parts = []
    for r in y_refs:
        if len(r.shape) == 3:
            parts.extend(r[j] for j in range(r.shape[0]))
        else:
            parts.append(r[...])
    y = parts[0] if len(parts) == 1 else jnp.concatenate(parts, axis=1)
    x = x_ref[...] + _dot(y, wmix_ref[...])
    h = _rms(x, g_ref[...]).astype(BF16)
    q = (_dot(h, wq_ref[...]) * (XA_HEAD_DIM ** -0.5)).astype(BF16)
    outs = []
    for hh in range(XA_HEADS):
        cols = slice(hh * XA_HEAD_DIM, (hh + 1) * XA_HEAD_DIM)
        s = _dot_nt(q[:, cols], k_ref[:, cols])
        p = jnp.exp(s - jnp.max(s, axis=-1, keepdims=True))
        p = (p / jnp.sum(p, axis=-1, keepdims=True)).astype(BF16)
        outs.append(_dot(p, v_ref[:, cols]).astype(BF16))
    o_ref[...] = x + _dot(jnp.concatenate(outs, axis=1), wo_ref[...])


def _xattn(x, ys, w_mix, mix_layer, g, wq, k, v, wo, layer, *, tm=1024):
    s = x.shape[0]
    hbm = pl.BlockSpec(memory_space=pl.ANY)
    assert w_mix.shape[1:] == wq.shape[1:] == wo.shape[1:] == (D_MODEL, D_MODEL)
    wbuf = pltpu.VMEM((D_MODEL, D_MODEL), BF16)
    row = pl.BlockSpec((tm, D_MODEL), lambda i: (i, 0))
    y_specs = []
    for y in ys:
        if y.ndim == 3:
            y_specs.append(pl.BlockSpec((y.shape[0], tm, y.shape[2]), lambda i: (0, i, 0)))
        else:
            y_specs.append(pl.BlockSpec((tm, y.shape[1]), lambda i: (i, 0)))
    return pl.pallas_call(
        functools.partial(_xattn_body, n_mix=len(ys), mix_layer=mix_layer, layer=layer),
        out_shape=jax.ShapeDtypeStruct((s, D_MODEL), F32),
        grid=(s // tm,),
        in_specs=[row] + y_specs + [hbm, _resident((1, D_MODEL)), hbm, _resident(k.shape),
                                    _resident(v.shape), hbm],
        out_specs=row,
        scratch_shapes=[wbuf, wbuf, wbuf,
                        pltpu.VMEM((2, D_MODEL // WEIGHT_STAGE_CHUNKS, D_MODEL), F32),
                        pltpu.SemaphoreType.DMA((2,))],
        compiler_params=_cparams(("arbitrary",)),
        name="mix_out_cross_attn",
    )(x, *ys, w_mix, g, wq, k, v, wo)


def _dnprep_body(x_ref, g_ref, w_hbm, wab_ref, wabt_ref, cw_ref,
                 q_ref, k_ref, v_ref, z_ref, ab_ref, abt_ref, ext_ref, w_ref, stage, sem, *, tm, layer):
    i = pl.program_id(0)
    width = 3 * DN_WIDTH

    @pl.when(i == 0)
    def _():
        ext_ref[0:CONV_PAD] = jnp.zeros((CONV_PAD, width), F32)
        _stage_weight(w_hbm, layer, w_ref, stage, sem)

    h = _rms(x_ref[...], g_ref[...]).astype(BF16)
    outs = (q_ref, k_ref, v_ref)
    qkv_blocks = width // MXU_DIM
    z_blocks = DN_WIDTH // MXU_DIM
    for b in range(qkv_blocks):
        if b % (qkv_blocks // z_blocks) == 0:
            zc = (b // (qkv_blocks // z_blocks)) * MXU_DIM
            z_ref[:, zc:zc + MXU_DIM] = _dot(h, w_ref[:, width + zc:width + zc + MXU_DIM])
        c0 = b * MXU_DIM
        cols = slice(c0, c0 + MXU_DIM)
        ext_ref[CONV_PAD:CONV_PAD + tm, cols] = _dot(h, w_ref[:, cols])
        cw = cw_ref[:, cols]
        for r0 in range(0, tm, DN_PREP_ROWS):
            act = _silu(_causal_conv(ext_ref[r0:r0 + DN_PREP_ROWS + CONV_PAD, cols], cw, DN_PREP_ROWS))
            for c1 in range(c0, c0 + MXU_DIM, DN_HEAD_DIM):
                part, hh = divmod(c1 // DN_HEAD_DIM, DN_HEADS)
                u = act[:, c1 - c0:c1 - c0 + DN_HEAD_DIM]
                if part < 2:
                    u = u * lax.rsqrt(jnp.sum(u * u, axis=-1, keepdims=True) + L2_EPS)
                if part == 0:
                    u = u * (DN_HEAD_DIM ** -0.5)
                outs[part][hh, r0:r0 + DN_PREP_ROWS, :] = u.astype(BF16)
    ext_ref[0:CONV_PAD] = ext_ref[tm:tm + CONV_PAD]

    ab_ref[...] = _dot(h, wab_ref[...])
    abt = _dot_nt(wabt_ref[...], h)
    for j in range(tm // DN_CHUNK):
        abt_ref[j] = abt[:, j * DN_CHUNK:(j + 1) * DN_CHUNK]


def _dnprep(x, g, w, layer, wab, wabt, cw, *, tm=256):
    s = x.shape[0]
    row = pl.BlockSpec((tm, D_MODEL), lambda i: (i, 0))
    head = pl.BlockSpec((DN_HEADS, tm, DN_HEAD_DIM), lambda i: (0, i, 0))
    head_shape = jax.ShapeDtypeStruct((DN_HEADS, s, DN_HEAD_DIM), BF16)
    nab = wabt.shape[0]
    return pl.pallas_call(
        functools.partial(_dnprep_body, tm=tm, layer=layer),
        out_shape=(head_shape, head_shape, head_shape,
                   jax.ShapeDtypeStruct((s, DN_WIDTH), F32),
                   jax.ShapeDtypeStruct((s, LANES), F32),
                   jax.ShapeDtypeStruct((s // DN_CHUNK, nab, DN_CHUNK), F32)),
        grid=(s // tm,),
        in_specs=[row, _resident((1, D_MODEL)), pl.BlockSpec(memory_space=pl.ANY),
                  _resident(wab.shape), _resident(wabt.shape), _resident(cw.shape)],
        out_specs=(head, head, head, row,
                   pl.BlockSpec((tm, LANES), lambda i: (i, 0)),
                   pl.BlockSpec((tm // DN_CHUNK, nab, DN_CHUNK), lambda i: (i, 0, 0))),
        scratch_shapes=[pltpu.VMEM((tm + CONV_PAD, 3 * DN_WIDTH), F32)] + _staged_weight_scratch(w.shape[1:]),
        compiler_params=_cparams(("arbitrary",)),
        name="dn_prep",
    )(x, g, w, wab, wabt, cw)


def _dn_body(q_ref, k_ref, v_ref, z_ref, ab_ref, abt_ref, prow_ref, pcol_ref, on_ref, y_ref, state, *, ts):
    c = DN_CHUNK

    @pl.when(pl.program_id(0) == 0)
    def _():
        state[...] = jnp.zeros((DN_HEADS, DN_HEAD_DIM, DN_HEAD_DIM), F32)

    ri = lax.broadcasted_iota(jnp.int32, (c, c), 0)
    ci = lax.broadcasted_iota(jnp.int32, (c, c), 1)
    lower = ri >= ci
    strict = ri > ci
    tri_l = lower.astype(F32)
    tri_u = (ri <= ci).astype(F32)
    eye = (ri == ci).astype(F32)
    diag_blk = (ri // DN_INV_BLOCK) == (ci // DN_INV_BLOCK)
    off_blks = []
    bsz = DN_INV_BLOCK
    while bsz < c:
        off_blks.append(((ri // (2 * bsz)) == (ci // (2 * bsz))) & ((ri // bsz) != (ci // bsz)))
        bsz *= 2
    hi = lax.Precision.HIGHEST
    a_log_row, dtb_row = prow_ref[0:1, :], prow_ref[1:2, :]
    a_log_col, dtb_col = pcol_ref[:, 0:1], pcol_ref[:, 1:2]
    onorm = on_ref[...]

    heads = range(DN_HEADS)

    def step(it, _):
        rows, gcol, beta, decay, egc, kk, qq, kf = [], [], [], [], [], [], [], []
        for u in range(DN_UNROLL):
            ci_ = it * DN_UNROLL + u
            r = pl.ds(pl.multiple_of(ci_ * c, c), c)
            ab = ab_ref[r, :]
            abt = abt_ref[ci_]
            g_col = -jnp.exp(a_log_row) * _softplus(ab + dtb_row)
            gc_col = jnp.dot(tri_l, g_col, precision=hi, preferred_element_type=F32)
            beta_all = _sigmoid(ab)
            g_row = -jnp.exp(a_log_col) * _softplus(abt + dtb_col)
            gc_row = jnp.dot(g_row, tri_u, precision=hi, preferred_element_type=F32)
            for h in heads:
                rows.append(r)
                gcol.append(gc_col[:, h:h + 1])
                beta.append(beta_all[:, DN_HEADS + h:DN_HEADS + h + 1])
                decay.append(jnp.where(lower, jnp.exp(jnp.minimum(gcol[-1] - gc_row[h:h + 1, :], 0.0)), 0.0))
                egc.append(jnp.exp(gcol[-1]))
                kk.append(k_ref[h, r, :])
                qq.append(q_ref[h, r, :])
                kf.append(kk[-1].astype(F32))
        items = range(DN_UNROLL * DN_HEADS)
        kb = [kf[i] * beta[i] for i in items]
        m = [jnp.where(strict, _dot_nt(kb[i].astype(BF16), kk[i]) * decay[i], 0.0) * -1.0 for i in items]
        qk = [jnp.where(lower, _dot_nt(qq[i], kk[i]) * decay[i], 0.0).astype(BF16) for i in items]
        pw = [jnp.where(diag_blk, m[i], 0.0) for i in items]
        t_inv = [eye + pw[i] for i in items]
        for _ in range(int(math.log2(DN_INV_BLOCK)) - 1):
            pwb = [pw[i].astype(BF16) for i in items]
            pw = [_dot(pwb[i], pwb[i]) for i in items]
            t_inv = [t_inv[i] + _dot(t_inv[i].astype(BF16), pw[i].astype(BF16)) for i in items]
        for off_blk in off_blks:
            tb = [t_inv[i].astype(BF16) for i in items]
            x = [_dot(jnp.where(off_blk, m[i], 0.0).astype(BF16), tb[i]).astype(BF16) for i in items]
            t_inv = [t_inv[i] + _dot(tb[i], x[i]) for i in items]
        rhs = [jnp.concatenate([v_ref[i % DN_HEADS, rows[i], :].astype(F32) * beta[i], kb[i] * egc[i]],
                               axis=1).astype(BF16) for i in items]
        sol = [_dot(t_inv[i].astype(BF16), rhs[i]) for i in items]
        qg = [(qq[i].astype(F32) * egc[i]).astype(BF16) for i in items]
        kd, eg_last = [], []
        for i in items:
            glast = gcol[i][c - 1:c, :]
            kd.append((kf[i] * jnp.exp(glast - gcol[i])).astype(BF16))
            eg_last.append(jnp.exp(glast))
        for u in range(DN_UNROLL):
            idx = [u * DN_HEADS + h for h in heads]
            st = [state[h] for h in heads]
            stb = [st[h].astype(BF16) for h in heads]
            v_new = [sol[i][:, :DN_HEAD_DIM] - _dot(sol[i][:, DN_HEAD_DIM:].astype(BF16), stb[h])
                     for h, i in enumerate(idx)]
            vnb = [v.astype(BF16) for v in v_new]
            for h, i in enumerate(idx):
                state[h] = st[h] * eg_last[i] + _dot_tn(kd[i], vnb[h])
            o = [_dot(qg[i], stb[h]) + _dot(qk[i], vnb[h]) for h, i in enumerate(idx)]
            for h, i in enumerate(idx):
                cols = slice(h * DN_HEAD_DIM, (h + 1) * DN_HEAD_DIM)
                on = o[h] * lax.rsqrt(jnp.mean(o[h] * o[h], axis=-1, keepdims=True) + NORM_EPS) * onorm
                y_ref[rows[i], cols] = (on * _silu(z_ref[rows[i], cols])).astype(BF16)
        return 0

    lax.fori_loop(0, ts // (c * DN_UNROLL), step, 0)


def _deltanet(q, k, v, z, ab, abt, prow, pcol, onorm, *, ts=1024):
    s = z.shape[0]
    head = pl.BlockSpec((DN_HEADS, ts, DN_HEAD_DIM), lambda i: (0, i, 0))
    nab = abt.shape[1]
    return pl.pallas_call(
        functools.partial(_dn_body, ts=ts),
        out_shape=jax.ShapeDtypeStruct((s, DN_WIDTH), BF16),
        grid=(s // ts,),
        in_specs=[head, head, head,
                  pl.BlockSpec((ts, DN_WIDTH), lambda i: (i, 0)),
                  pl.BlockSpec((ts, LANES), lambda i: (i, 0)),
                  pl.BlockSpec((ts // DN_CHUNK, nab, DN_CHUNK), lambda i: (i, 0, 0)),
                  _resident(prow.shape), _resident(pcol.shape), _resident(onorm.shape)],
        out_specs=pl.BlockSpec((ts, DN_WIDTH), lambda i: (i, 0)),
        scratch_shapes=[pltpu.VMEM((DN_HEADS, DN_HEAD_DIM, DN_HEAD_DIM), F32)],
        compiler_params=_cparams(("arbitrary",)),
        name="gated_deltanet",
    )(q, k, v, z, ab, abt, prow, pcol, onorm)


def _row(v):
    return v.reshape(1, -1).astype(F32)


def _block_diag(w):
    n, j, k = w.shape
    eye = jnp.eye(n, dtype=w.dtype)
    return (w[:, :, None, :] * eye[:, None, :, None]).reshape(n * j, n * k)


def kernel(x, mem, ffn1_norm, ffn1_w_in, ffn1_w_out, mix_norm, xa_norm, xa_mem_norm, xa_wq, xa_wkv, xa_wo,
           ffn2_norm, ffn2_w_in, ffn2_w_out, ab_w_in, lru_conv_w, lru_conv_b, lru_w_a, lru_b_a, lru_w_x,
           lru_b_x, lru_lambda, ab_w_out, dn_w_in, dn_conv_w, dn_a_log, dn_dt_bias, dn_o_norm, dn_w_out,
           final_norm):
    batch, seq, _ = x.shape
    depth = ffn1_norm.shape[0]
    outs = []
    for bi in range(batch):
        xs = x[bi]
        ms = mem[bi]
        for layer in range(depth):
            j = layer // 2
            last = layer == depth - 1
            xs = _ffn(xs, _row(ffn1_norm[layer]), ffn1_w_in, ffn1_w_out, _row(final_norm), layer,
                      final_norm=False)
            if layer % 2 == 0:
                q, k, v, xr, gr = _proj0(xs, _row(mix_norm[layer]), ab_w_in, j)
                attn = _attention(q, k, v)
                y = _lru(xr, gr, lru_conv_w[j].astype(F32), _row(lru_conv_b[j]),
                         _block_diag(lru_w_a[j]).astype(BF16), _row(lru_b_a[j]),
                         _block_diag(lru_w_x[j]).astype(BF16), _row(lru_b_x[j]), _row(lru_lambda[j]))
                mix_ys, mix_w = (attn, y), ab_w_out
            else:
                w = dn_w_in[j]
                wab = w[:, 4 * DN_WIDTH:]
                n_ab = wab.shape[1]
                wab_pad = jnp.pad(wab, ((0, 0), (0, LANES - n_ab))).astype(BF16)
                q, k, v, z, ab, abt = _dnprep(
                    xs, _row(mix_norm[layer]), dn_w_in, j, wab_pad, wab.T.astype(BF16),
                    dn_conv_w[j].astype(F32))
                pad = jnp.zeros((LANES - DN_HEADS,), F32)
                prow = jnp.stack([jnp.concatenate([dn_a_log[j].astype(F32), pad]),
                                  jnp.concatenate([dn_dt_bias[j].astype(F32), pad])])
                pad2 = jnp.zeros((n_ab - DN_HEADS,), F32)
                pcol = jnp.stack([jnp.concatenate([dn_a_log[j].astype(F32), pad2]),
                                  jnp.concatenate([dn_dt_bias[j].astype(F32), pad2])], axis=1)
                y = _deltanet(q, k, v, z, ab, abt, prow, pcol, _row(dn_o_norm[j]))
                mix_ys, mix_w = (y,), dn_w_out
            mk, mv = _memkv(ms, _row(xa_mem_norm[layer]), xa_wkv[layer].astype(BF16))
            xs = _xattn(xs, mix_ys, mix_w, j, _row(xa_norm[layer]), xa_wq, mk, mv, xa_wo, layer)
            xs = _ffn(xs, _row(ffn2_norm[layer]), ffn2_w_in, ffn2_w_out, _row(final_norm), layer,
                      final_norm=last)
        outs.append(xs)
    return outs[0][None] if batch == 1 else jnp.stack(outs, axis=0)
```

```python
import functools
import math

import jax
import jax.numpy as jnp
from jax import lax
from jax.experimental import pallas as pl
from jax.experimental.pallas import tpu as pltpu

F32 = jnp.float32
BF16 = jnp.bfloat16

NORM_EPS = 1e-6
L2_EPS = 1e-6
LANES = 128
SUBLANES = 8

D_MODEL = 1024
D_FF = 2816
MXU_DIM = 256
FF_CHUNKS = (6 * MXU_DIM, 5 * MXU_DIM)
assert sum(FF_CHUNKS) == D_FF
WEIGHT_STAGE_CHUNKS = 8
ATTN_HEADS = 8
ATTN_HEAD_DIM = 64
ATTN_WIDTH = 512
ATTN_PAIRS = ATTN_WIDTH // LANES
DILATIONS = (1, 4, 16)
N_BACK = 128
ATTN_BLOCK = 128
ATTN_TILE = 2048
LRU_WIDTH = 512
LRU_C = 8.0
CONV_K = 4
CONV_PAD = SUBLANES
DN_HEADS = 8
DN_HEAD_DIM = 128
DN_WIDTH = 1024
DN_CHUNK = 128
DN_INV_BLOCK = 16
DN_UNROLL = 4
DN_PREP_ROWS = 64
XA_HEADS = 4
XA_HEAD_DIM = 256
NEG_BIG = -1e30
LOG2_E = math.log2(math.e)
VMEM_LIMIT = 56 * 1024 * 1024


def _cparams(sem):
    return pltpu.CompilerParams(dimension_semantics=sem, vmem_limit_bytes=VMEM_LIMIT)


def _resident(shape, layer=None):
    if layer is None:
        nd = len(shape)
        return pl.BlockSpec(shape, lambda *_: (0,) * nd, pipeline_mode=pl.Buffered(1))
    nd = len(shape) - 1
    return pl.BlockSpec((None,) + tuple(shape[1:]), lambda *_: (layer,) + (0,) * nd,
                        pipeline_mode=pl.Buffered(1))


def _dot(a, b):
    return jnp.dot(a, b, preferred_element_type=F32)


def _dot_nt(a, b):
    return lax.dot_general(a, b, (((1,), (1,)), ((), ())), preferred_element_type=F32)


def _dot_tn(a, b):
    return lax.dot_general(a, b, (((0,), (0,)), ((), ())), preferred_element_type=F32)


def _rms(x, g):
    return x * lax.rsqrt(jnp.mean(x * x, axis=-1, keepdims=True) + NORM_EPS) * g


def _sigmoid(x):
    return 1.0 / (1.0 + jnp.exp(-x))


def _silu(x):
    return x * _sigmoid(x)


def _softplus(x):
    return jnp.maximum(x, 0.0) + jnp.log1p(jnp.exp(-jnp.abs(x)))


def _causal_conv(ext, w, tm):
    y = w[CONV_K - 1:CONV_K, :] * ext[CONV_PAD:CONV_PAD + tm]
    for shift in range(1, CONV_K):
        k = CONV_K - 1 - shift
        y = y + w[k:k + 1, :] * pltpu.roll(ext, shift, 0)[CONV_PAD:CONV_PAD + tm]
    return y


def _gelu_tanh(x):
    c = math.sqrt(2.0 / math.pi)
    return 0.5 * x * (1.0 + jnp.tanh(c * (x + 0.044715 * (x * x * x))))


def _stage_weight(src_hbm, layer, dst, stage, sem):
    rows = stage.shape[1]
    n = dst.shape[0] // rows

    def copy(c):
        return pltpu.make_async_copy(src_hbm.at[layer, pl.ds(c * rows, rows), :], stage.at[c % 2], sem.at[c % 2])

    copy(0).start()
    for c in range(n):
        if c + 1 < n:
            copy(c + 1).start()
        copy(c).wait()
        dst[c * rows:(c + 1) * rows, :] = stage[c % 2].astype(BF16)


def _staged_weight_scratch(shape):
    rows, cols = shape
    return [pltpu.VMEM((rows, cols), BF16), pltpu.VMEM((2, rows // WEIGHT_STAGE_CHUNKS, cols), F32),
            pltpu.SemaphoreType.DMA((2,))]


def _ffn_body(x_ref, g_ref, win_hbm, wout_hbm, fg_ref, o_ref, win_ref, sin, sem_in, wout_ref, sout, sem_out,
              *, layer, final_norm):
    @pl.when(pl.program_id(0) == 0)
    def _():
        _stage_weight(win_hbm, layer, win_ref, sin, sem_in)
        _stage_weight(wout_hbm, layer, wout_ref, sout, sem_out)

    x = x_ref[...]
    h = _rms(x, g_ref[...]).astype(BF16)
    acc = jnp.zeros_like(x)
    lo = 0
    for width in FF_CHUNKS:
        gate = _dot(h, win_ref[:, lo:lo + width])
        up = _dot(h, win_ref[:, D_FF + lo:D_FF + lo + width])
        a = (_silu(gate) * up).astype(BF16)
        acc = acc + _dot(a, wout_ref[lo:lo + width, :])
        lo += width
    y = x + 0.5 * acc
    if final_norm:
        y = _rms(y, fg_ref[...])
    o_ref[...] = y


def _ffn(x, g, w_in, w_out, fg, layer, *, final_norm, tm=512):
    s = x.shape[0]
    row = pl.BlockSpec((tm, D_MODEL), lambda i: (i, 0))
    hbm = pl.BlockSpec(memory_space=pl.ANY)
    return pl.pallas_call(
        functools.partial(_ffn_body, layer=layer, final_norm=final_norm),
        out_shape=jax.ShapeDtypeStruct((s, D_MODEL), F32),
        grid=(s // tm,),
        in_specs=[row, _resident((1, D_MODEL)), hbm, hbm, _resident((1, D_MODEL))],
        out_specs=row,
        scratch_shapes=_staged_weight_scratch(w_in.shape[1:]) + _staged_weight_scratch(w_out.shape[1:]),
        compiler_params=_cparams(("arbitrary",)),
        name="ffn_final" if final_norm else "ffn",
    )(x, g, w_in, w_out, fg)


def _proj0_body(x_ref, g_ref, w_hbm, *refs, tm, layer):
    nd = len(DILATIONS)
    qkv_refs = (refs[0:nd], refs[nd:2 * nd], refs[2 * nd:3 * nd])
    xr_ref, gr_ref = refs[3 * nd], refs[3 * nd + 1]
    w_ref, stage, sem = refs[3 * nd + 2:]

    @pl.when(pl.program_id(0) == 0)
    def _():
        _stage_weight(w_hbm, layer, w_ref, stage, sem)

    h = _rms(x_ref[...], g_ref[...]).astype(BF16)
    p = _dot(h, w_ref[...])
    xr_ref[...] = p[:, 3 * ATTN_WIDTH:3 * ATTN_WIDTH + LRU_WIDTH]
    gr_ref[...] = p[:, 3 * ATTN_WIDTH + LRU_WIDTH:]
    scale = ATTN_HEAD_DIM ** -0.5 * LOG2_E
    qkv = jnp.concatenate([p[:, :ATTN_WIDTH] * scale, p[:, ATTN_WIDTH:3 * ATTN_WIDTH]], axis=1).astype(BF16)

    ri = lax.broadcasted_iota(jnp.int32, (MXU_DIM, MXU_DIM), 0)
    ci = lax.broadcasted_iota(jnp.int32, (MXU_DIM, MXU_DIM), 1)
    for gi, d in enumerate(DILATIONS):
        per = MXU_DIM // d
        perm = (ci == (ri % per) * d + ri // per).astype(BF16) if d > 1 else None
        for c in range(tm // MXU_DIM):
            chunk = qkv[c * MXU_DIM:(c + 1) * MXU_DIM]
            if perm is not None:
                chunk = _dot(perm, chunk).astype(BF16)
            for r in range(d):
                piece = chunk[r * per:(r + 1) * per]
                for a in range(3):
                    for j in range(ATTN_PAIRS):
                        col = a * ATTN_WIDTH + j * LANES
                        qkv_refs[a][gi][j, c * per:(c + 1) * per, r * LANES:(r + 1) * LANES] = (
                            piece[:, col:col + LANES])


def _proj0(x, g, w, layer, *, tm=1024):
    s = x.shape[0]
    nd = len(DILATIONS)
    row = pl.BlockSpec((tm, D_MODEL), lambda i: (i, 0))
    half = pl.BlockSpec((tm, LRU_WIDTH), lambda i: (i, 0))
    view_shapes = [jax.ShapeDtypeStruct((ATTN_PAIRS, s // d, d * LANES), BF16) for d in DILATIONS]
    view_specs = [pl.BlockSpec((ATTN_PAIRS, tm // d, d * LANES), lambda i: (0, i, 0)) for d in DILATIONS]
    half_shape = jax.ShapeDtypeStruct((s, LRU_WIDTH), F32)
    outs = pl.pallas_call(
        functools.partial(_proj0_body, tm=tm, layer=layer),
        out_shape=tuple(view_shapes * 3) + (half_shape, half_shape),
        grid=(s // tm,),
        in_specs=[row, _resident((1, D_MODEL)), pl.BlockSpec(memory_space=pl.ANY)],
        out_specs=tuple(view_specs * 3) + (half, half),
        scratch_shapes=_staged_weight_scratch(w.shape[1:]),
        compiler_params=_cparams(("arbitrary",)),
        name="proj0",
    )(x, g, w)
    return outs[0:nd], outs[nd:2 * nd], outs[2 * nd:3 * nd], outs[3 * nd], outs[3 * nd + 1]


def _attn_body(*refs):
    nd = len(DILATIONS)
    q_refs, k_refs, v_refs = refs[0:nd], refs[nd:2 * nd], refs[2 * nd:3 * nd]
    o_ref = refs[3 * nd]
    khist, vhist = refs[3 * nd + 1:4 * nd + 1], refs[4 * nd + 1:5 * nd + 1]
    og_ref, lg_ref = refs[5 * nd + 1], refs[5 * nd + 2]
    t = pl.program_id(1)

    @pl.when(t == 0)
    def _():
        for h_ref in khist + vhist:
            h_ref[...] = jnp.zeros(h_ref.shape, BF16)

    nk = N_BACK + ATTN_BLOCK
    qi = lax.broadcasted_iota(jnp.int32, (ATTN_BLOCK, nk), 0)
    kj = lax.broadcasted_iota(jnp.int32, (ATTN_BLOCK, nk), 1)
    dist = qi + N_BACK - kj
    band = (dist >= 0) & (dist <= N_BACK)
    bias = jnp.where(band, 0.0, NEG_BIG)
    bias_first = jnp.where(band & (kj >= jnp.where(t == 0, N_BACK, 0)), 0.0, NEG_BIG)
    lane = lax.broadcasted_iota(jnp.int32, (1, LANES), 1)
    head_mask = [(lane < ATTN_HEAD_DIM).astype(BF16), (lane >= ATTN_HEAD_DIM).astype(BF16)]
    lane_lo = lax.broadcasted_iota(jnp.int32, (ATTN_BLOCK, LANES), 1) < ATTN_HEAD_DIM

    for gi, d in enumerate(DILATIONS):
        q_ref, k_ref, v_ref = q_refs[gi], k_refs[gi], v_refs[gi]
        for r in range(d):
            lanes = slice(r * LANES, (r + 1) * LANES)
            for sub in range(ATTN_TILE // d // ATTN_BLOCK):
                lo = ATTN_BLOCK * sub
                q = q_ref[0, lo:lo + ATTN_BLOCK, lanes]
                if sub == 0:
                    kk = jnp.concatenate([khist[gi][:, lanes], k_ref[0, 0:ATTN_BLOCK, lanes]], axis=0)
                    vv = jnp.concatenate([vhist[gi][:, lanes], v_ref[0, 0:ATTN_BLOCK, lanes]], axis=0)
                else:
                    kk = k_ref[0, lo - N_BACK:lo + ATTN_BLOCK, lanes]
                    vv = v_ref[0, lo - N_BACK:lo + ATTN_BLOCK, lanes]
                o_pair = None
                l_pair = None
                for hp in range(2):
                    s = _dot_nt(q * head_mask[hp], kk) + (bias_first if sub == 0 else bias)
                    m = jnp.max(s, axis=-1, keepdims=True)
                    p = jnp.exp2(s - m)
                    den = jnp.sum(p, axis=-1, keepdims=True)
                    o = _dot(p.astype(BF16), vv) / den
                    lse = jnp.broadcast_to(m + jnp.log2(den), (ATTN_BLOCK, LANES))
                    o_pair = o if hp == 0 else jnp.where(lane_lo, o_pair, o)
                    l_pair = lse if hp == 0 else jnp.where(lane_lo, l_pair, lse)
                tok_rows = pl.ds(r + d * lo, ATTN_BLOCK, stride=d)
                og_ref[gi, tok_rows, :] = o_pair
                lg_ref[gi, tok_rows, :] = l_pair
        rows = ATTN_TILE // d
        khist[gi][...] = k_ref[0, rows - N_BACK:rows, :]
        vhist[gi][...] = v_ref[0, rows - N_BACK:rows, :]

    l0, l1, l2 = lg_ref[0], lg_ref[1], lg_ref[2]
    lm = jnp.maximum(jnp.maximum(l0, l1), l2)
    w0, w1, w2 = jnp.exp2(l0 - lm), jnp.exp2(l1 - lm), jnp.exp2(l2 - lm)
    out = (w0 * og_ref[0] + w1 * og_ref[1] + w2 * og_ref[2]) / (w0 + w1 + w2)
    o_ref[0] = out.astype(BF16)


def _attention(q_views, k_views, v_views):
    s = q_views[0].shape[1]
    views = list(q_views) + list(k_views) + list(v_views)
    specs = [pl.BlockSpec((1, ATTN_TILE // d, d * LANES), lambda p, t: (p, t, 0)) for d in DILATIONS] * 3
    hist = [pltpu.VMEM((N_BACK, d * LANES), BF16) for d in DILATIONS]
    return pl.pallas_call(
        _attn_body,
        out_shape=jax.ShapeDtypeStruct((ATTN_PAIRS, s, LANES), BF16),
        grid=(ATTN_PAIRS, s // ATTN_TILE),
        in_specs=specs,
        out_specs=pl.BlockSpec((1, ATTN_TILE, LANES), lambda p, t: (p, t, 0)),
        scratch_shapes=hist + hist + [pltpu.VMEM((len(DILATIONS), ATTN_TILE, LANES), F32),
                                      pltpu.VMEM((len(DILATIONS), ATTN_TILE, LANES), F32)],
        compiler_params=_cparams(("arbitrary", "arbitrary")),
        name="dilated_attn",
    )(*views)


def _scan_step(a, b, k, pos):
    ok = pos >= k
    a_sh = pltpu.roll(a, k, 0)
    b_sh = pltpu.roll(b, k, 0)
    return jnp.where(ok, a * a_sh, a), jnp.where(ok, a * b_sh + b, b)


def _lru_body(xr_ref, gr_ref, cw_ref, cb_ref, wa_ref, ba_ref, wx_ref, bx_ref, lam_ref, y_ref,
              ext_ref, carry_ref, a_s, b_s, cin_s, *, tm):
    i = pl.program_id(0)

    @pl.when(i == 0)
    def _():
        ext_ref[0:CONV_PAD] = jnp.zeros((CONV_PAD, LRU_WIDTH), F32)
        carry_ref[...] = jnp.zeros((SUBLANES, LRU_WIDTH), F32)

    @pl.when(i > 0)
    def _():
        ext_ref[0:CONV_PAD] = ext_ref[tm:tm + CONV_PAD]

    ext_ref[CONV_PAD:CONV_PAD + tm] = xr_ref[...]
    xc = _causal_conv(ext_ref[...], cw_ref[...], tm) + cb_ref[...]
    xcb = xc.astype(BF16)
    r = _sigmoid(_dot(xcb, wa_ref[...]) + ba_ref[...])
    ig = _sigmoid(_dot(xcb, wx_ref[...]) + bx_ref[...])
    log_a = -LRU_C * r * _softplus(-lam_ref[...])
    a = jnp.exp(log_a)
    b = jnp.sqrt(-jnp.tanh(log_a) * (a * a + 1.0)) * ig * xc

    a = a.reshape(tm // SUBLANES, SUBLANES, LRU_WIDTH)
    b = b.reshape(tm // SUBLANES, SUBLANES, LRU_WIDTH)
    pos = lax.broadcasted_iota(jnp.int32, a.shape, 1)
    for k in (1, 2, 4):
        ok = pos >= k
        a_sh = pltpu.roll(a, k, 1)
        b_sh = pltpu.roll(b, k, 1)
        a, b = jnp.where(ok, a * a_sh, a), jnp.where(ok, a * b_sh + b, b)
    a = a.reshape(tm, LRU_WIDTH)
    b = b.reshape(tm, LRU_WIDTH)
    lane_groups = [slice(g * LANES, (g + 1) * LANES) for g in range(LRU_WIDTH // LANES)]
    for g, cols in enumerate(lane_groups):
        a_s[g] = a[:, cols]
        b_s[g] = b[:, cols]
    ng = tm // SUBLANES
    ends = pl.ds(SUBLANES - 1, ng, stride=SUBLANES)
    ae = jnp.concatenate([a_s[g, ends, :] for g in range(len(lane_groups))], axis=1)
    be = jnp.concatenate([b_s[g, ends, :] for g in range(len(lane_groups))], axis=1)
    gpos = lax.broadcasted_iota(jnp.int32, (ng, LRU_WIDTH), 0)
    k = 1
    while k < ng:
        ae, be = _scan_step(ae, be, k, gpos)
        k *= 2
    carry = carry_ref[0:1, :]
    h_end = ae * carry + be
    h_in = jnp.where(gpos >= 1, pltpu.roll(h_end, 1, 0), carry)
    for g, cols in enumerate(lane_groups):
        for j in range(SUBLANES):
            cin_s[g, pl.ds(j, ng, stride=SUBLANES), :] = h_in[:, cols]
    carry_ref[...] = jnp.broadcast_to(h_end[ng - 1:ng, :], (SUBLANES, LRU_WIDTH))
    h = jnp.concatenate([a_s[g] * cin_s[g] + b_s[g] for g in range(len(lane_groups))], axis=1)
    y_ref[...] = (h * _gelu_tanh(gr_ref[...])).astype(BF16)


def _lru(xr, gr, cw, cb, wa, ba, wx, bx, lam, *, tm=512):
    s = xr.shape[0]
    row = pl.BlockSpec((tm, LRU_WIDTH), lambda i: (i, 0))
    vec = _resident((1, LRU_WIDTH))
    sq = _resident((LRU_WIDTH, LRU_WIDTH))
    return pl.pallas_call(
        functools.partial(_lru_body, tm=tm),
        out_shape=jax.ShapeDtypeStruct((s, LRU_WIDTH), BF16),
        grid=(s // tm,),
        in_specs=[row, row, _resident((CONV_K, LRU_WIDTH)), vec, sq, vec, sq, vec, vec],
        out_specs=row,
        scratch_shapes=[pltpu.VMEM((tm + CONV_PAD, LRU_WIDTH), F32),
                        pltpu.VMEM((SUBLANES, LRU_WIDTH), F32),
                        pltpu.VMEM((LRU_WIDTH // LANES, tm, LANES), F32),
                        pltpu.VMEM((LRU_WIDTH // LANES, tm, LANES), F32),
                        pltpu.VMEM((LRU_WIDTH // LANES, tm, LANES), F32)],
        compiler_params=_cparams(("arbitrary",)),
        name="rglru",
    )(xr, gr, cw, cb, wa, ba, wx, bx, lam)


def _xattn_body(x_ref, *refs, n_mix, mix_layer, layer):
    y_refs, wmix_hbm = refs[:n_mix], refs[n_mix]
    (g_ref, wq_hbm, mem_ref, gm_ref, wkv_hbm, wo_hbm, o_ref, wmix_ref, wq_ref, wo_ref, stage, sem,
     wkv_ref, kv_stage, kv_sem, k_ref, v_ref) = refs[n_mix + 1:]

    @pl.when(pl.program_id(0) == 0)
    def _():
        _stage_weight(wmix_hbm, mix_layer, wmix_ref, stage, sem)
        _stage_weight(wq_hbm, layer, wq_ref, stage, sem)
        _stage_weight(wo_hbm, layer, wo_ref, stage, sem)
        _stage_weight(wkv_hbm, layer, wkv_ref, kv_stage, kv_sem)
        kv = _dot(_rms(mem_ref[...], gm_ref[...]).astype(BF16), wkv_ref[...])
        k_ref[...] = kv[:, :D_MODEL].astype(BF16)
        v_ref[...] = kv[:, D_MODEL:].astype(BF16)

    parts = []
    for r in y_refs:
        if len(r.shape) == 3:
            parts.extend(r[j] for j in range(r.shape[0]))
        else:
            parts.append(r[...])
    y = parts[0] if len(parts) == 1 else jnp.concatenate(parts, axis=1)
    x = x_ref[...] + _dot(y, wmix_ref[...])
    h = _rms(x, g_ref[...]).astype(BF16)
    q = (_dot(h, wq_ref[...]) * (XA_HEAD_DIM ** -0.5)).astype(BF16)
    outs = []
    for hh in range(XA_HEADS):
        cols = slice(hh * XA_HEAD_DIM, (hh + 1) * XA_HEAD_DIM)
        s = _dot_nt(q[:, cols], k_ref[:, cols])
        p = jnp.exp(s - jnp.max(s, axis=-1, keepdims=True))
        p = (p / jnp.sum(p, axis=-1, keepdims=True)).astype(BF16)
        outs.append(_dot(p, v_ref[:, cols]).astype(BF16))
    o_ref[...] = x + _dot(jnp.concatenate(outs, axis=1), wo_ref[...])


def _xattn(x, ys, w_mix, mix_layer, g, wq, mem, g_mem, wkv, wo, layer, *, tm=1024):
    s = x.shape[0]
    n_mem = mem.shape[0]
    kv_buf = pltpu.VMEM((n_mem, D_MODEL), BF16)
    hbm = pl.BlockSpec(memory_space=pl.ANY)
    assert w_mix.shape[1:] == wq.shape[1:] == wo.shape[1:] == (D_MODEL, D_MODEL)
    wbuf = pltpu.VMEM((D_MODEL, D_MODEL), BF16)
    row = pl.BlockSpec((tm, D_MODEL), lambda i: (i, 0))
    y_specs = []
    for y in ys:
        if y.ndim == 3:
            y_specs.append(pl.BlockSpec((y.shape[0], tm, y.shape[2]), lambda i: (0, i, 0)))
        else:
            y_specs.append(pl.BlockSpec((tm, y.shape[1]), lambda i: (i, 0)))
    return pl.pallas_call(
        functools.partial(_xattn_body, n_mix=len(ys), mix_layer=mix_layer, layer=layer),
        out_shape=jax.ShapeDtypeStruct((s, D_MODEL), F32),
        grid=(s // tm,),
        in_specs=[row] + y_specs + [hbm, _resident((1, D_MODEL)), hbm, _resident(mem.shape),
                                    _resident((1, D_MODEL)), hbm, hbm],
        out_specs=row,
        scratch_shapes=[wbuf, wbuf, wbuf,
                        pltpu.VMEM((2, D_MODEL // WEIGHT_STAGE_CHUNKS, D_MODEL), F32),
                        pltpu.SemaphoreType.DMA((2,))] + _staged_weight_scratch(wkv.shape[1:]) + [kv_buf, kv_buf],
        compiler_params=_cparams(("arbitrary",)),
        name="mix_out_cross_attn",
    )(x, *ys, w_mix, g, wq, mem, g_mem, wkv, wo)


def _dnprep_body(x_ref, g_ref, w_hbm, wab_ref, wabt_ref, cw_ref,
                 q_ref, k_ref, v_ref, z_ref, ab_ref, abt_ref, ext_ref, w_ref, stage, sem, *, tm, layer):
    i = pl.program_id(0)
    width = 3 * DN_WIDTH

    @pl.when(i == 0)
    def _():
        ext_ref[0:CONV_PAD] = jnp.zeros((CONV_PAD, width), F32)
        _stage_weight(w_hbm, layer, w_ref, stage, sem)

    h = _rms(x_ref[...], g_ref[...]).astype(BF16)
    outs = (q_ref, k_ref, v_ref)
    qkv_blocks = width // MXU_DIM
    z_blocks = DN_WIDTH // MXU_DIM
    for b in range(qkv_blocks):
        if b % (qkv_blocks // z_blocks) == 0:
            zc = (b // (qkv_blocks // z_blocks)) * MXU_DIM
            z_ref[:, zc:zc + MXU_DIM] = _dot(h, w_ref[:, width + zc:width + zc + MXU_DIM])
        c0 = b * MXU_DIM
        cols = slice(c0, c0 + MXU_DIM)
        ext_ref[CONV_PAD:CONV_PAD + tm, cols] = _dot(h, w_ref[:, cols])
        cw = cw_ref[:, cols]
        for r0 in range(0, tm, DN_PREP_ROWS):
            act = _silu(_causal_conv(ext_ref[r0:r0 + DN_PREP_ROWS + CONV_PAD, cols], cw, DN_PREP_ROWS))
            for c1 in range(c0, c0 + MXU_DIM, DN_HEAD_DIM):
                part, hh = divmod(c1 // DN_HEAD_DIM, DN_HEADS)
                u = act[:, c1 - c0:c1 - c0 + DN_HEAD_DIM]
                if part < 2:
                    u = u * lax.rsqrt(jnp.sum(u * u, axis=-1, keepdims=True) + L2_EPS)
                if part == 0:
                    u = u * (DN_HEAD_DIM ** -0.5)
                outs[part][hh, r0:r0 + DN_PREP_ROWS, :] = u.astype(BF16)
    ext_ref[0:CONV_PAD] = ext_ref[tm:tm + CONV_PAD]

    ab_ref[...] = _dot(h, wab_ref[...])
    abt = _dot_nt(wabt_ref[...], h)
    for j in range(tm // DN_CHUNK):
        abt_ref[j] = abt[:, j * DN_CHUNK:(j + 1) * DN_CHUNK]


def _dnprep(x, g, w, layer, wab, wabt, cw, *, tm=256):
    s = x.shape[0]
    row = pl.BlockSpec((tm, D_MODEL), lambda i: (i, 0))
    head = pl.BlockSpec((DN_HEADS, tm, DN_HEAD_DIM), lambda i: (0, i, 0))
    head_shape = jax.ShapeDtypeStruct((DN_HEADS, s, DN_HEAD_DIM), BF16)
    nab = wabt.shape[0]
    return pl.pallas_call(
        functools.partial(_dnprep_body, tm=tm, layer=layer),
        out_shape=(head_shape, head_shape, head_shape,
                   jax.ShapeDtypeStruct((s, DN_WIDTH), F32),
                   jax.ShapeDtypeStruct((s, LANES), F32),
                   jax.ShapeDtypeStruct((s // DN_CHUNK, nab, DN_CHUNK), F32)),
        grid=(s // tm,),
        in_specs=[row, _resident((1, D_MODEL)), pl.BlockSpec(memory_space=pl.ANY),
                  _resident(wab.shape), _resident(wabt.shape), _resident(cw.shape)],
        out_specs=(head, head, head, row,
                   pl.BlockSpec((tm, LANES), lambda i: (i, 0)),
                   pl.BlockSpec((tm // DN_CHUNK, nab, DN_CHUNK), lambda i: (i, 0, 0))),
        scratch_shapes=[pltpu.VMEM((tm + CONV_PAD, 3 * DN_WIDTH), F32)] + _staged_weight_scratch(w.shape[1:]),
        compiler_params=_cparams(("arbitrary",)),
        name="dn_prep",
    )(x, g, w, wab, wabt, cw)


def _dn_body(q_ref, k_ref, v_ref, z_ref, ab_ref, abt_ref, prow_ref, pcol_ref, on_ref, y_ref, state, *, ts):
    c = DN_CHUNK

    @pl.when(pl.program_id(0) == 0)
    def _():
        state[...] = jnp.zeros((DN_HEADS, DN_HEAD_DIM, DN_HEAD_DIM), F32)

    ri = lax.broadcasted_iota(jnp.int32, (c, c), 0)
    ci = lax.broadcasted_iota(jnp.int32, (c, c), 1)
    lower = ri >= ci
    strict = ri > ci
    tri_l = lower.astype(F32)
    tri_u = (ri <= ci).astype(F32)
    eye = (ri == ci).astype(F32)
    diag_blk = (ri // DN_INV_BLOCK) == (ci // DN_INV_BLOCK)
    off_blks = []
    bsz = DN_INV_BLOCK
    while bsz < c:
        off_blks.append(((ri // (2 * bsz)) == (ci // (2 * bsz))) & ((ri // bsz) != (ci // bsz)))
        bsz *= 2
    hi = lax.Precision.HIGHEST
    a_log_row, dtb_row = prow_ref[0:1, :], prow_ref[1:2, :]
    a_log_col, dtb_col = pcol_ref[:, 0:1], pcol_ref[:, 1:2]
    onorm = on_ref[...]

    heads = range(DN_HEADS)

    def step(it, _):
        rows, gcol, beta, decay, egc, kk, qq, kf = [], [], [], [], [], [], [], []
        for u in range(DN_UNROLL):
            ci_ = it * DN_UNROLL + u
            r = pl.ds(pl.multiple_of(ci_ * c, c), c)
            ab = ab_ref[r, :]
            abt = abt_ref[ci_]
            g_col = -jnp.exp(a_log_row) * _softplus(ab + dtb_row)
            gc_col = jnp.dot(tri_l, g_col, precision=hi, preferred_element_type=F32)
            beta_all = _sigmoid(ab)
            g_row = -jnp.exp(a_log_col) * _softplus(abt + dtb_col)
            gc_row = jnp.dot(g_row, tri_u, precision=hi, preferred_element_type=F32)
            for h in heads:
                rows.append(r)
                gcol.append(gc_col[:, h:h + 1])
                beta.append(beta_all[:, DN_HEADS + h:DN_HEADS + h + 1])
                decay.append(jnp.where(lower, jnp.exp(jnp.minimum(gcol[-1] - gc_row[h:h + 1, :], 0.0)), 0.0))
                egc.append(jnp.exp(gcol[-1]))
                kk.append(k_ref[h, r, :])
                qq.append(q_ref[h, r, :])
                kf.append(kk[-1].astype(F32))
        items = range(DN_UNROLL * DN_HEADS)
        kb = [kf[i] * beta[i] for i in items]
        m = [jnp.where(strict, _dot_nt(kb[i].astype(BF16), kk[i]) * decay[i], 0.0) * -1.0 for i in items]
        qk = [jnp.where(lower, _dot_nt(qq[i], kk[i]) * decay[i], 0.0).astype(BF16) for i in items]
        pw = [jnp.where(diag_blk, m[i], 0.0) for i in items]
        t_inv = [eye + pw[i] for i in items]
        for _ in range(int(math.log2(DN_INV_BLOCK)) - 1):
            pwb = [pw[i].astype(BF16) for i in items]
            pw = [_dot(pwb[i], pwb[i]) for i in items]
            t_inv = [t_inv[i] + _dot(t_inv[i].astype(BF16), pw[i].astype(BF16)) for i in items]
        for off_blk in off_blks:
            tb = [t_inv[i].astype(BF16) for i in items]
            x = [_dot(jnp.where(off_blk, m[i], 0.0).astype(BF16), tb[i]).astype(BF16) for i in items]
            t_inv = [t_inv[i] + _dot(tb[i], x[i]) for i in items]
        rhs = [jnp.concatenate([v_ref[i % DN_HEADS, rows[i], :].astype(F32) * beta[i], kb[i] * egc[i]],
                               axis=1).astype(BF16) for i in items]
        sol = [_dot(t_inv[i].astype(BF16), rhs[i]) for i in items]
        qg = [(qq[i].astype(F32) * egc[i]).astype(BF16) for i in items]
        kd, eg_last = [], []
        for i in items:
            glast = gcol[i][c - 1:c, :]
            kd.append((kf[i] * jnp.exp(glast - gcol[i])).astype(BF16))
            eg_last.append(jnp.exp(glast))
        for u in range(DN_UNROLL):
            idx = [u * DN_HEADS + h for h in heads]
            st = [state[h] for h in heads]
            stb = [st[h].astype(BF16) for h in heads]
            v_new = [sol[i][:, :DN_HEAD_DIM] - _dot(sol[i][:, DN_HEAD_DIM:].astype(BF16), stb[h])
                     for h, i in enumerate(idx)]
            vnb = [v.astype(BF16) for v in v_new]
            for h, i in enumerate(idx):
                state[h] = st[h] * eg_last[i] + _dot_tn(kd[i], vnb[h])
            o = [_dot(qg[i], stb[h]) + _dot(qk[i], vnb[h]) for h, i in enumerate(idx)]
            for h, i in enumerate(idx):
                cols = slice(h * DN_HEAD_DIM, (h + 1) * DN_HEAD_DIM)
                on = o[h] * lax.rsqrt(jnp.mean(o[h] * o[h], axis=-1, keepdims=True) + NORM_EPS) * onorm
                y_ref[rows[i], cols] = (on * _silu(z_ref[rows[i], cols])).astype(BF16)
        return 0

    lax.fori_loop(0, ts // (c * DN_UNROLL), step, 0)


def _deltanet(q, k, v, z, ab, abt, prow, pcol, onorm, *, ts=1024):
    s = z.shape[0]
    head = pl.BlockSpec((DN_HEADS, ts, DN_HEAD_DIM), lambda i: (0, i, 0))
    nab = abt.shape[1]
    return pl.pallas_call(
        functools.partial(_dn_body, ts=ts),
        out_shape=jax.ShapeDtypeStruct((s, DN_WIDTH), BF16),
        grid=(s // ts,),
        in_specs=[head, head, head,
                  pl.BlockSpec((ts, DN_WIDTH), lambda i: (i, 0)),
                  pl.BlockSpec((ts, LANES), lambda i: (i, 0)),
                  pl.BlockSpec((ts // DN_CHUNK, nab, DN_CHUNK), lambda i: (i, 0, 0)),
                  _resident(prow.shape), _resident(pcol.shape), _resident(onorm.shape)],
        out_specs=pl.BlockSpec((ts, DN_WIDTH), lambda i: (i, 0)),
        scratch_shapes=[pltpu.VMEM((DN_HEADS, DN_HEAD_DIM, DN_HEAD_DIM), F32)],
        compiler_params=_cparams(("arbitrary",)),
        name="gated_deltanet",
    )(q, k, v, z, ab, abt, prow, pcol, onorm)


def _row(v):
    return v.reshape(1, -1).astype(F32)


def _block_diag(w):
    n, j, k = w.shape
    eye = jnp.eye(n, dtype=w.dtype)
    return (w[:, :, None, :] * eye[:, None, :, None]).reshape(n * j, n * k)


def kernel(x, mem, ffn1_norm, ffn1_w_in, ffn1_w_out, mix_norm, xa_norm, xa_mem_norm, xa_wq, xa_wkv, xa_wo,
           ffn2_norm, ffn2_w_in, ffn2_w_out, ab_w_in, lru_conv_w, lru_conv_b, lru_w_a, lru_b_a, lru_w_x,
           lru_b_x, lru_lambda, ab_w_out, dn_w_in, dn_conv_w, dn_a_log, dn_dt_bias, dn_o_norm, dn_w_out,
           final_norm):
    batch, seq, _ = x.shape
    depth = ffn1_norm.shape[0]
    outs = []
    for bi in range(batch):
        xs = x[bi]
        ms = mem[bi]
        for layer in range(depth):
            j = layer // 2
            last = layer == depth - 1
            xs = _ffn(xs, _row(ffn1_norm[layer]), ffn1_w_in, ffn1_w_out, _row(final_norm), layer,
                      final_norm=False)
            if layer % 2 == 0:
                q, k, v, xr, gr = _proj0(xs, _row(mix_norm[layer]), ab_w_in, j)
                attn = _attention(q, k, v)
                y = _lru(xr, gr, lru_conv_w[j].astype(F32), _row(lru_conv_b[j]),
                         _block_diag(lru_w_a[j]).astype(BF16), _row(lru_b_a[j]),
                         _block_diag(lru_w_x[j]).astype(BF16), _row(lru_b_x[j]), _row(lru_lambda[j]))
                mix_ys, mix_w = (attn, y), ab_w_out
            else:
                w = dn_w_in[j]
                wab = w[:, 4 * DN_WIDTH:]
                n_ab = wab.shape[1]
                wab_pad = jnp.pad(wab, ((0, 0), (0, LANES - n_ab))).astype(BF16)
                q, k, v, z, ab, abt = _dnprep(
                    xs, _row(mix_norm[layer]), dn_w_in, j, wab_pad, wab.T.astype(BF16),
                    dn_conv_w[j].astype(F32))
                pad = jnp.zeros((LANES - DN_HEADS,), F32)
                prow = jnp.stack([jnp.concatenate([dn_a_log[j].astype(F32), pad]),
                                  jnp.concatenate([dn_dt_bias[j].astype(F32), pad])])
                pad2 = jnp.zeros((n_ab - DN_HEADS,), F32)
                pcol = jnp.stack([jnp.concatenate([dn_a_log[j].astype(F32), pad2]),
                                  jnp.concatenate([dn_dt_bias[j].astype(F32), pad2])], axis=1)
                y = _deltanet(q, k, v, z, ab, abt, prow, pcol, _row(dn_o_norm[j]))
                mix_ys, mix_w = (y,), dn_w_out
            xs = _xattn(xs, mix_ys, mix_w, j, _row(xa_norm[layer]), xa_wq, ms, _row(xa_mem_norm[layer]),
                        xa_wkv, xa_wo, layer)
            xs = _ffn(xs, _row(ffn2_norm[layer]), ffn2_w_in, ffn2_w_out, _row(final_norm), layer,
                      final_norm=last)
        outs.append(xs)
    return outs[0][None] if batch == 1 else jnp.stack(outs, axis=0)
```

```python
import functools
import math

import jax
import jax.numpy as jnp
from jax import lax
from jax.experimental import pallas as pl
from jax.experimental.pallas import tpu as pltpu

F32 = jnp.float32
BF16 = jnp.bfloat16

NORM_EPS = 1e-6
L2_EPS = 1e-6
LANES = 128
SUBLANES = 8

D_MODEL = 1024
D_FF = 2816
MXU_DIM = 256
FF_CHUNKS = (6 * MXU_DIM, 5 * MXU_DIM)
assert sum(FF_CHUNKS) == D_FF
WEIGHT_STAGE_CHUNKS = 8
ATTN_HEADS = 8
ATTN_HEAD_DIM = 64
ATTN_WIDTH = 512
ATTN_PAIRS = ATTN_WIDTH // LANES
DILATIONS = (1, 4, 16)
N_BACK = 128
ATTN_BLOCK = 128
ATTN_TILE = 2048
LRU_WIDTH = 512
LRU_C = 8.0
CONV_K = 4
CONV_PAD = SUBLANES
DN_HEADS = 8
DN_HEAD_DIM = 128
DN_WIDTH = 1024
DN_CHUNK = 128
DN_INV_BLOCK = 16
DN_UNROLL = 4
DN_PREP_ROWS = 64
XA_HEADS = 4
XA_HEAD_DIM = 256
NEG_BIG = -1e30
LOG2_E = math.log2(math.e)
VMEM_LIMIT = 56 * 1024 * 1024


def _cparams(sem):
    return pltpu.CompilerParams(dimension_semantics=sem, vmem_limit_bytes=VMEM_LIMIT)


def _resident(shape, layer=None):
    if layer is None:
        nd = len(shape)
        return pl.BlockSpec(shape, lambda *_: (0,) * nd, pipeline_mode=pl.Buffered(1))
    nd = len(shape) - 1
    return pl.BlockSpec((None,) + tuple(shape[1:]), lambda *_: (layer,) + (0,) * nd,
                        pipeline_mode=pl.Buffered(1))


def _dot(a, b):
    return jnp.dot(a, b, preferred_element_type=F32)


def _dot_nt(a, b):
    return lax.dot_general(a, b, (((1,), (1,)), ((), ())), preferred_element_type=F32)


def _dot_tn(a, b):
    return lax.dot_general(a, b, (((0,), (0,)), ((), ())), preferred_element_type=F32)


def _rms(x, g):
    return x * lax.rsqrt(jnp.mean(x * x, axis=-1, keepdims=True) + NORM_EPS) * g


def _sigmoid(x):
    return 1.0 / (1.0 + jnp.exp(-x))


def _silu(x):
    return x * _sigmoid(x)


def _softplus(x):
    return jnp.maximum(x, 0.0) + jnp.log1p(jnp.exp(-jnp.abs(x)))


def _causal_conv(ext, w, tm):
    y = w[CONV_K - 1:CONV_K, :] * ext[CONV_PAD:CONV_PAD + tm]
    for shift in range(1, CONV_K):
        k = CONV_K - 1 - shift
        y = y + w[k:k + 1, :] * pltpu.roll(ext, shift, 0)[CONV_PAD:CONV_PAD + tm]
    return y


def _gelu_tanh(x):
    c = math.sqrt(2.0 / math.pi)
    return 0.5 * x * (1.0 + jnp.tanh(c * (x + 0.044715 * (x * x * x))))


def _stage_weight(src_hbm, layer, dst, stage, sem):
    rows = stage.shape[1]
    n = dst.shape[0] // rows

    def copy(c):
        return pltpu.make_async_copy(src_hbm.at[layer, pl.ds(c * rows, rows), :], stage.at[c % 2], sem.at[c % 2])

    copy(0).start()
    for c in range(n):
        if c + 1 < n:
            copy(c + 1).start()
        copy(c).wait()
        dst[c * rows:(c + 1) * rows, :] = stage[c % 2].astype(BF16)


def _staged_weight_scratch(shape):
    rows, cols = shape
    return [pltpu.VMEM((rows, cols), BF16), pltpu.VMEM((2, rows // WEIGHT_STAGE_CHUNKS, cols), F32),
            pltpu.SemaphoreType.DMA((2,))]


def _ffn_body(x_ref, g_ref, win_hbm, wout_hbm, fg_ref, o_ref, win_ref, sin, sem_in, wout_ref, sout, sem_out,
              *, layer, final_norm):
    @pl.when(pl.program_id(0) == 0)
    def _():
        _stage_weight(win_hbm, layer, win_ref, sin, sem_in)
        _stage_weight(wout_hbm, layer, wout_ref, sout, sem_out)

    x = x_ref[...]
    h = _rms(x, g_ref[...]).astype(BF16)
    acc = jnp.zeros_like(x)
    lo = 0
    for width in FF_CHUNKS:
        gate = _dot(h, win_ref[:, lo:lo + width])
        up = _dot(h, win_ref[:, D_FF + lo:D_FF + lo + width])
        a = (_silu(gate) * up).astype(BF16)
        acc = acc + _dot(a, wout_ref[lo:lo + width, :])
        lo += width
    y = x + 0.5 * acc
    if final_norm:
        y = _rms(y, fg_ref[...])
    o_ref[...] = y


def _ffn(x, g, w_in, w_out, fg, layer, *, final_norm, tm=512):
    s = x.shape[0]
    row = pl.BlockSpec((tm, D_MODEL), lambda i: (i, 0))
    hbm = pl.BlockSpec(memory_space=pl.ANY)
    return pl.pallas_call(
        functools.partial(_ffn_body, layer=layer, final_norm=final_norm),
        out_shape=jax.ShapeDtypeStruct((s, D_MODEL), F32),
        grid=(s // tm,),
        in_specs=[row, _resident((1, D_MODEL)), hbm, hbm, _resident((1, D_MODEL))],
        out_specs=row,
        scratch_shapes=_staged_weight_scratch(w_in.shape[1:]) + _staged_weight_scratch(w_out.shape[1:]),
        compiler_params=_cparams(("arbitrary",)),
        name="ffn_final" if final_norm else "ffn",
    )(x, g, w_in, w_out, fg)


def _proj0_body(x_ref, g_ref, w_hbm, *refs, tm, layer):
    nd = len(DILATIONS)
    qkv_refs = (refs[0:nd], refs[nd:2 * nd], refs[2 * nd:3 * nd])
    xr_ref, gr_ref = refs[3 * nd], refs[3 * nd + 1]
    w_ref, stage, sem = refs[3 * nd + 2:]

    @pl.when(pl.program_id(0) == 0)
    def _():
        _stage_weight(w_hbm, layer, w_ref, stage, sem)

    h = _rms(x_ref[...], g_ref[...]).astype(BF16)
    p = _dot(h, w_ref[...])
    xr_ref[...] = p[:, 3 * ATTN_WIDTH:3 * ATTN_WIDTH + LRU_WIDTH]
    gr_ref[...] = p[:, 3 * ATTN_WIDTH + LRU_WIDTH:]
    scale = ATTN_HEAD_DIM ** -0.5 * LOG2_E
    qkv = jnp.concatenate([p[:, :ATTN_WIDTH] * scale, p[:, ATTN_WIDTH:3 * ATTN_WIDTH]], axis=1).astype(BF16)

    ri = lax.broadcasted_iota(jnp.int32, (MXU_DIM, MXU_DIM), 0)
    ci = lax.broadcasted_iota(jnp.int32, (MXU_DIM, MXU_DIM), 1)
    for gi, d in enumerate(DILATIONS):
        per = MXU_DIM // d
        perm = (ci == (ri % per) * d + ri // per).astype(BF16) if d > 1 else None
        for c in range(tm // MXU_DIM):
            chunk = qkv[c * MXU_DIM:(c + 1) * MXU_DIM]
            if perm is not None:
                chunk = _dot(perm, chunk).astype(BF16)
            for r in range(d):
                piece = chunk[r * per:(r + 1) * per]
                for a in range(3):
                    for j in range(ATTN_PAIRS):
                        col = a * ATTN_WIDTH + j * LANES
                        qkv_refs[a][gi][j, c * per:(c + 1) * per, r * LANES:(r + 1) * LANES] = (
                            piece[:, col:col + LANES])


def _proj0(x, g, w, layer, *, tm=1024):
    s = x.shape[0]
    nd = len(DILATIONS)
    row = pl.BlockSpec((tm, D_MODEL), lambda i: (i, 0))
    half = pl.BlockSpec((tm, LRU_WIDTH), lambda i: (i, 0))
    view_shapes = [jax.ShapeDtypeStruct((ATTN_PAIRS, s // d, d * LANES), BF16) for d in DILATIONS]
    view_specs = [pl.BlockSpec((ATTN_PAIRS, tm // d, d * LANES), lambda i: (0, i, 0)) for d in DILATIONS]
    half_shape = jax.ShapeDtypeStruct((s, LRU_WIDTH), F32)
    outs = pl.pallas_call(
        functools.partial(_proj0_body, tm=tm, layer=layer),
        out_shape=tuple(view_shapes * 3) + (half_shape, half_shape),
        grid=(s // tm,),
        in_specs=[row, _resident((1, D_MODEL)), pl.BlockSpec(memory_space=pl.ANY)],
        out_specs=tuple(view_specs * 3) + (half, half),
        scratch_shapes=_staged_weight_scratch(w.shape[1:]),
        compiler_params=_cparams(("arbitrary",)),
        name="proj0",
    )(x, g, w)
    return outs[0:nd], outs[nd:2 * nd], outs[2 * nd:3 * nd], outs[3 * nd], outs[3 * nd + 1]


def _attn_body(*refs):
    nd = len(DILATIONS)
    q_refs, k_refs, v_refs = refs[0:nd], refs[nd:2 * nd], refs[2 * nd:3 * nd]
    o_ref = refs[3 * nd]
    khist, vhist = refs[3 * nd + 1:4 * nd + 1], refs[4 * nd + 1:5 * nd + 1]
    og_ref, lg_ref = refs[5 * nd + 1], refs[5 * nd + 2]
    t = pl.program_id(1)

    @pl.when(t == 0)
    def _():
        for h_ref in khist + vhist:
            h_ref[...] = jnp.zeros(h_ref.shape, BF16)

    nk = N_BACK + ATTN_BLOCK
    qi = lax.broadcasted_iota(jnp.int32, (ATTN_BLOCK, nk), 0)
    kj = lax.broadcasted_iota(jnp.int32, (ATTN_BLOCK, nk), 1)
    dist = qi + N_BACK - kj
    band = (dist >= 0) & (dist <= N_BACK)
    bias = jnp.where(band, 0.0, NEG_BIG)
    bias_first = jnp.where(band & (kj >= jnp.where(t == 0, N_BACK, 0)), 0.0, NEG_BIG)
    lane = lax.broadcasted_iota(jnp.int32, (1, LANES), 1)
    head_mask = [(lane < ATTN_HEAD_DIM).astype(BF16), (lane >= ATTN_HEAD_DIM).astype(BF16)]
    lane_lo = lax.broadcasted_iota(jnp.int32, (ATTN_BLOCK, LANES), 1) < ATTN_HEAD_DIM

    for gi, d in enumerate(DILATIONS):
        q_ref, k_ref, v_ref = q_refs[gi], k_refs[gi], v_refs[gi]
        for r in range(d):
            lanes = slice(r * LANES, (r + 1) * LANES)
            for sub in range(ATTN_TILE // d // ATTN_BLOCK):
                lo = ATTN_BLOCK * sub
                q = q_ref[0, lo:lo + ATTN_BLOCK, lanes]
                if sub == 0:
                    kk = jnp.concatenate([khist[gi][:, lanes], k_ref[0, 0:ATTN_BLOCK, lanes]], axis=0)
                    vv = jnp.concatenate([vhist[gi][:, lanes], v_ref[0, 0:ATTN_BLOCK, lanes]], axis=0)
                else:
                    kk = k_ref[0, lo - N_BACK:lo + ATTN_BLOCK, lanes]
                    vv = v_ref[0, lo - N_BACK:lo + ATTN_BLOCK, lanes]
                o_pair = None
                l_pair = None
                for hp in range(2):
                    s = _dot_nt(q * head_mask[hp], kk) + (bias_first if sub == 0 else bias)
                    m = jnp.max(s, axis=-1, keepdims=True)
                    p = jnp.exp2(s - m)
                    den = jnp.sum(p, axis=-1, keepdims=True)
                    o = _dot(p.astype(BF16), vv) / den
                    lse = jnp.broadcast_to(m + jnp.log2(den), (ATTN_BLOCK, LANES))
                    o_pair = o if hp == 0 else jnp.where(lane_lo, o_pair, o)
                    l_pair = lse if hp == 0 else jnp.where(lane_lo, l_pair, lse)
                tok_rows = pl.ds(r + d * lo, ATTN_BLOCK, stride=d)
                og_ref[gi, tok_rows, :] = o_pair
                lg_ref[gi, tok_rows, :] = l_pair
        rows = ATTN_TILE // d
        khist[gi][...] = k_ref[0, rows - N_BACK:rows, :]
        vhist[gi][...] = v_ref[0, rows - N_BACK:rows, :]

    l0, l1, l2 = lg_ref[0], lg_ref[1], lg_ref[2]
    lm = jnp.maximum(jnp.maximum(l0, l1), l2)
    w0, w1, w2 = jnp.exp2(l0 - lm), jnp.exp2(l1 - lm), jnp.exp2(l2 - lm)
    out = (w0 * og_ref[0] + w1 * og_ref[1] + w2 * og_ref[2]) / (w0 + w1 + w2)
    o_ref[0] = out.astype(BF16)


def _attention(q_views, k_views, v_views):
    s = q_views[0].shape[1]
    views = list(q_views) + list(k_views) + list(v_views)
    specs = [pl.BlockSpec((1, ATTN_TILE // d, d * LANES), lambda p, t: (p, t, 0)) for d in DILATIONS] * 3
    hist = [pltpu.VMEM((N_BACK, d * LANES), BF16) for d in DILATIONS]
    return pl.pallas_call(
        _attn_body,
        out_shape=jax.ShapeDtypeStruct((ATTN_PAIRS, s, LANES), BF16),
        grid=(ATTN_PAIRS, s // ATTN_TILE),
        in_specs=specs,
        out_specs=pl.BlockSpec((1, ATTN_TILE, LANES), lambda p, t: (p, t, 0)),
        scratch_shapes=hist + hist + [pltpu.VMEM((len(DILATIONS), ATTN_TILE, LANES), F32),
                                      pltpu.VMEM((len(DILATIONS), ATTN_TILE, LANES), F32)],
        compiler_params=_cparams(("arbitrary", "arbitrary")),
        name="dilated_attn",
    )(*views)


def _scan_step(a, b, k, pos):
    ok = pos >= k
    a_sh = pltpu.roll(a, k, 0)
    b_sh = pltpu.roll(b, k, 0)
    return jnp.where(ok, a * a_sh, a), jnp.where(ok, a * b_sh + b, b)


def _lru_body(xr_ref, gr_ref, cw_ref, cb_ref, wa_ref, ba_ref, wx_ref, bx_ref, lam_ref, y_ref,
              ext_ref, carry_ref, a_s, b_s, cin_s, *, tm):
    i = pl.program_id(0)

    @pl.when(i == 0)
    def _():
        ext_ref[0:CONV_PAD] = jnp.zeros((CONV_PAD, LRU_WIDTH), F32)
        carry_ref[...] = jnp.zeros((SUBLANES, LRU_WIDTH), F32)

    @pl.when(i > 0)
    def _():
        ext_ref[0:CONV_PAD] = ext_ref[tm:tm + CONV_PAD]

    ext_ref[CONV_PAD:CONV_PAD + tm] = xr_ref[...]
    xc = _causal_conv(ext_ref[...], cw_ref[...], tm) + cb_ref[...]
    xcb = xc.astype(BF16)
    r = _sigmoid(_dot(xcb, wa_ref[...]) + ba_ref[...])
    ig = _sigmoid(_dot(xcb, wx_ref[...]) + bx_ref[...])
    log_a = -LRU_C * r * _softplus(-lam_ref[...])
    a = jnp.exp(log_a)
    b = jnp.sqrt(-jnp.tanh(log_a) * (a * a + 1.0)) * ig * xc

    a = a.reshape(tm // SUBLANES, SUBLANES, LRU_WIDTH)
    b = b.reshape(tm // SUBLANES, SUBLANES, LRU_WIDTH)
    pos = lax.broadcasted_iota(jnp.int32, a.shape, 1)
    for k in (1, 2, 4):
        ok = pos >= k
        a_sh = pltpu.roll(a, k, 1)
        b_sh = pltpu.roll(b, k, 1)
        a, b = jnp.where(ok, a * a_sh, a), jnp.where(ok, a * b_sh + b, b)
    a = a.reshape(tm, LRU_WIDTH)
    b = b.reshape(tm, LRU_WIDTH)
    lane_groups = [slice(g * LANES, (g + 1) * LANES) for g in range(LRU_WIDTH // LANES)]
    for g, cols in enumerate(lane_groups):
        a_s[g] = a[:, cols]
        b_s[g] = b[:, cols]
    ng = tm // SUBLANES
    ends = pl.ds(SUBLANES - 1, ng, stride=SUBLANES)
    ae = jnp.concatenate([a_s[g, ends, :] for g in range(len(lane_groups))], axis=1)
    be = jnp.concatenate([b_s[g, ends, :] for g in range(len(lane_groups))], axis=1)
    gpos = lax.broadcasted_iota(jnp.int32, (ng, LRU_WIDTH), 0)
    k = 1
    while k < ng:
        ae, be = _scan_step(ae, be, k, gpos)
        k *= 2
    carry = carry_ref[0:1, :]
    h_end = ae * carry + be
    h_in = jnp.where(gpos >= 1, pltpu.roll(h_end, 1, 0), carry)
    for g, cols in enumerate(lane_groups):
        for j in range(SUBLANES):
            cin_s[g, pl.ds(j, ng, stride=SUBLANES), :] = h_in[:, cols]
    carry_ref[...] = jnp.broadcast_to(h_end[ng - 1:ng, :], (SUBLANES, LRU_WIDTH))
    h = jnp.concatenate([a_s[g] * cin_s[g] + b_s[g] for g in range(len(lane_groups))], axis=1)
    y_ref[...] = (h * _gelu_tanh(gr_ref[...])).astype(BF16)


def _lru(xr, gr, cw, cb, wa, ba, wx, bx, lam, *, tm=512):
    s = xr.shape[0]
    row = pl.BlockSpec((tm, LRU_WIDTH), lambda i: (i, 0))
    vec = _resident((1, LRU_WIDTH))
    sq = _resident((LRU_WIDTH, LRU_WIDTH))
    return pl.pallas_call(
        functools.partial(_lru_body, tm=tm),
        out_shape=jax.ShapeDtypeStruct((s, LRU_WIDTH), BF16),
        grid=(s // tm,),
        in_specs=[row, row, _resident((CONV_K, LRU_WIDTH)), vec, sq, vec, sq, vec, vec],
        out_specs=row,
        scratch_shapes=[pltpu.VMEM((tm + CONV_PAD, LRU_WIDTH), F32),
                        pltpu.VMEM((SUBLANES, LRU_WIDTH), F32),
                        pltpu.VMEM((LRU_WIDTH // LANES, tm, LANES), F32),
                        pltpu.VMEM((LRU_WIDTH // LANES, tm, LANES), F32),
                        pltpu.VMEM((LRU_WIDTH // LANES, tm, LANES), F32)],
        compiler_params=_cparams(("arbitrary",)),
        name="rglru",
    )(xr, gr, cw, cb, wa, ba, wx, bx, lam)


def _xattn_body(x_ref, *refs, n_mix, mix_layer, layer):
    y_refs, wmix_hbm = refs[:n_mix], refs[n_mix]
    (g_ref, wq_hbm, mem_ref, gm_ref, wkv_hbm, wo_hbm, o_ref, wmix_ref, wq_ref, wo_ref, stage, sem,
     wkv_ref, kv_stage, kv_sem, k_ref, v_ref) = refs[n_mix + 1:]

    @pl.when(pl.program_id(0) == 0)
    def _():
        _stage_weight(wmix_hbm, mix_layer, wmix_ref, stage, sem)
        _stage_weight(wq_hbm, layer, wq_ref, stage, sem)
        _stage_weight(wo_hbm, layer, wo_ref, stage, sem)
        _stage_weight(wkv_hbm, layer, wkv_ref, kv_stage, kv_sem)
        kv = _dot(_rms(mem_ref[...], gm_ref[...]).astype(BF16), wkv_ref[...])
        k_ref[...] = kv[:, :D_MODEL].astype(BF16)
        v_ref[...] = kv[:, D_MODEL:].astype(BF16)

    parts = []
    for r in y_refs:
        if len(r.shape) == 3:
            parts.extend(r[j] for j in range(r.shape[0]))
        else:
            parts.append(r[...])
    y = parts[0] if len(parts) == 1 else jnp.concatenate(parts, axis=1)
    x = x_ref[...] + _dot(y, wmix_ref[...])
    h = _rms(x, g_ref[...]).astype(BF16)
    q = (_dot(h, wq_ref[...]) * (XA_HEAD_DIM ** -0.5)).astype(BF16)
    outs = []
    for hh in range(XA_HEADS):
        cols = slice(hh * XA_HEAD_DIM, (hh + 1) * XA_HEAD_DIM)
        s = _dot_nt(q[:, cols], k_ref[:, cols])
        p = jnp.exp(s - jnp.max(s, axis=-1, keepdims=True))
        p = (p / jnp.sum(p, axis=-1, keepdims=True)).astype(BF16)
        outs.append(_dot(p, v_ref[:, cols]).astype(BF16))
    o_ref[...] = x + _dot(jnp.concatenate(outs, axis=1), wo_ref[...])


def _xattn(x, ys, w_mix, mix_layer, g, wq, mem, g_mem, wkv, wo, layer, *, tm=1024):
    s = x.shape[0]
    n_mem = mem.shape[0]
    kv_buf = pltpu.VMEM((n_mem, D_MODEL), BF16)
    hbm = pl.BlockSpec(memory_space=pl.ANY)
    assert w_mix.shape[1:] == wq.shape[1:] == wo.shape[1:] == (D_MODEL, D_MODEL)
    wbuf = pltpu.VMEM((D_MODEL, D_MODEL), BF16)
    row = pl.BlockSpec((tm, D_MODEL), lambda i: (i, 0))
    y_specs = []
    for y in ys:
        if y.ndim == 3:
            y_specs.append(pl.BlockSpec((y.shape[0], tm, y.shape[2]), lambda i: (0, i, 0)))
        else:
            y_specs.append(pl.BlockSpec((tm, y.shape[1]), lambda i: (i, 0)))
    return pl.pallas_call(
        functools.partial(_xattn_body, n_mix=len(ys), mix_layer=mix_layer, layer=layer),
        out_shape=jax.ShapeDtypeStruct((s, D_MODEL), F32),
        grid=(s // tm,),
        in_specs=[row] + y_specs + [hbm, _resident((1, D_MODEL)), hbm, _resident(mem.shape),
                                    _resident((1, D_MODEL)), hbm, hbm],
        out_specs=row,
        scratch_shapes=[wbuf, wbuf, wbuf,
                        pltpu.VMEM((2, D_MODEL // WEIGHT_STAGE_CHUNKS, D_MODEL), F32),
                        pltpu.SemaphoreType.DMA((2,))] + _staged_weight_scratch(wkv.shape[1:]) + [kv_buf, kv_buf],
        compiler_params=_cparams(("arbitrary",)),
        name="mix_out_cross_attn",
    )(x, *ys, w_mix, g, wq, mem, g_mem, wkv, wo)


def _dnprep_body(x_ref, g_ref, w_hbm, wab_ref, wabt_ref, cw_ref,
                 q_ref, k_ref, v_ref, z_ref, ab_ref, abt_ref, ext_ref, w_ref, stage, sem, *, tm, layer):
    i = pl.program_id(0)
    width = 3 * DN_WIDTH

    @pl.when(i == 0)
    def _():
        ext_ref[0:CONV_PAD] = jnp.zeros((CONV_PAD, width), F32)
        _stage_weight(w_hbm, layer, w_ref, stage, sem)

    h = _rms(x_ref[...], g_ref[...]).astype(BF16)
    outs = (q_ref, k_ref, v_ref)
    qkv_blocks = width // MXU_DIM
    z_blocks = DN_WIDTH // MXU_DIM
    for b in range(qkv_blocks):
        if b % (qkv_blocks // z_blocks) == 0:
            zc = (b // (qkv_blocks // z_blocks)) * MXU_DIM
            z_ref[:, zc:zc + MXU_DIM] = _dot(h, w_ref[:, width + zc:width + zc + MXU_DIM])
        c0 = b * MXU_DIM
        cols = slice(c0, c0 + MXU_DIM)
        ext_ref[CONV_PAD:CONV_PAD + tm, cols] = _dot(h, w_ref[:, cols])
        cw = cw_ref[:, cols]
        for r0 in range(0, tm, DN_PREP_ROWS):
            act = _silu(_causal_conv(ext_ref[r0:r0 + DN_PREP_ROWS + CONV_PAD, cols], cw, DN_PREP_ROWS))
            for c1 in range(c0, c0 + MXU_DIM, DN_HEAD_DIM):
                part, hh = divmod(c1 // DN_HEAD_DIM, DN_HEADS)
                u = act[:, c1 - c0:c1 - c0 + DN_HEAD_DIM]
                if part < 2:
                    inv = lax.rsqrt(jnp.sum(u * u, axis=-1, keepdims=True) + L2_EPS)
                    u = u * (inv * (DN_HEAD_DIM ** -0.5) if part == 0 else inv)
                outs[part][hh, r0:r0 + DN_PREP_ROWS, :] = u.astype(BF16)
    ext_ref[0:CONV_PAD] = ext_ref[tm:tm + CONV_PAD]

    ab_ref[...] = _dot(h, wab_ref[...])
    abt = _dot_nt(wabt_ref[...], h)
    for j in range(tm // DN_CHUNK):
        abt_ref[j] = abt[:, j * DN_CHUNK:(j + 1) * DN_CHUNK]


def _dnprep(x, g, w, layer, wab, wabt, cw, *, tm=256):
    s = x.shape[0]
    row = pl.BlockSpec((tm, D_MODEL), lambda i: (i, 0))
    head = pl.BlockSpec((DN_HEADS, tm, DN_HEAD_DIM), lambda i: (0, i, 0))
    head_shape = jax.ShapeDtypeStruct((DN_HEADS, s, DN_HEAD_DIM), BF16)
    nab = wabt.shape[0]
    return pl.pallas_call(
        functools.partial(_dnprep_body, tm=tm, layer=layer),
        out_shape=(head_shape, head_shape, head_shape,
                   jax.ShapeDtypeStruct((s, DN_WIDTH), F32),
                   jax.ShapeDtypeStruct((s, LANES), F32),
                   jax.ShapeDtypeStruct((s // DN_CHUNK, nab, DN_CHUNK), F32)),
        grid=(s // tm,),
        in_specs=[row, _resident((1, D_MODEL)), pl.BlockSpec(memory_space=pl.ANY),
                  _resident(wab.shape), _resident(wabt.shape), _resident(cw.shape)],
        out_specs=(head, head, head, row,
                   pl.BlockSpec((tm, LANES), lambda i: (i, 0)),
                   pl.BlockSpec((tm // DN_CHUNK, nab, DN_CHUNK), lambda i: (i, 0, 0))),
        scratch_shapes=[pltpu.VMEM((tm + CONV_PAD, 3 * DN_WIDTH), F32)] + _staged_weight_scratch(w.shape[1:]),
        compiler_params=_cparams(("arbitrary",)),
        name="dn_prep",
    )(x, g, w, wab, wabt, cw)


def _dn_body(q_ref, k_ref, v_ref, z_ref, ab_ref, abt_ref, prow_ref, pcol_ref, on_ref, y_ref, state, *, ts):
    c = DN_CHUNK

    @pl.when(pl.program_id(0) == 0)
    def _():
        state[...] = jnp.zeros((DN_HEADS, DN_HEAD_DIM, DN_HEAD_DIM), F32)

    ri = lax.broadcasted_iota(jnp.int32, (c, c), 0)
    ci = lax.broadcasted_iota(jnp.int32, (c, c), 1)
    lower = ri >= ci
    strict = ri > ci
    tri_l = lower.astype(F32)
    tri_u = (ri <= ci).astype(F32)
    eye = (ri == ci).astype(F32)
    diag_blk = (ri // DN_INV_BLOCK) == (ci // DN_INV_BLOCK)
    off_blks = []
    bsz = DN_INV_BLOCK
    while bsz < c:
        off_blks.append(((ri // (2 * bsz)) == (ci // (2 * bsz))) & ((ri // bsz) != (ci // bsz)))
        bsz *= 2
    hi = lax.Precision.HIGHEST
    a_log_row, dtb_row = prow_ref[0:1, :], prow_ref[1:2, :]
    a_log_col, dtb_col = pcol_ref[:, 0:1], pcol_ref[:, 1:2]
    onorm = on_ref[...]

    heads = range(DN_HEADS)

    def step(it, _):
        rows, gcol, beta, decay, egc, kk, qq, kf = [], [], [], [], [], [], [], []
        for u in range(DN_UNROLL):
            ci_ = it * DN_UNROLL + u
            r = pl.ds(pl.multiple_of(ci_ * c, c), c)
            ab = ab_ref[r, :]
            abt = abt_ref[ci_]
            g_col = -jnp.exp(a_log_row) * _softplus(ab + dtb_row)
            gc_col = jnp.dot(tri_l, g_col, precision=hi, preferred_element_type=F32)
            beta_all = _sigmoid(ab)
            g_row = -jnp.exp(a_log_col) * _softplus(abt + dtb_col)
            gc_row = jnp.dot(g_row, tri_u, precision=hi, preferred_element_type=F32)
            for h in heads:
                rows.append(r)
                gcol.append(gc_col[:, h:h + 1])
                beta.append(beta_all[:, DN_HEADS + h:DN_HEADS + h + 1])
                decay.append(jnp.where(lower, jnp.exp(jnp.minimum(gcol[-1] - gc_row[h:h + 1, :], 0.0)), 0.0))
                egc.append(jnp.exp(gcol[-1]))
                kk.append(k_ref[h, r, :])
                qq.append(q_ref[h, r, :])
                kf.append(kk[-1].astype(F32))
        items = range(DN_UNROLL * DN_HEADS)
        kb = [kf[i] * beta[i] for i in items]
        m = [jnp.where(strict, _dot_nt(kb[i].astype(BF16), kk[i]) * decay[i], 0.0) * -1.0 for i in items]
        qk = [jnp.where(lower, _dot_nt(qq[i], kk[i]) * decay[i], 0.0).astype(BF16) for i in items]
        pw = [jnp.where(diag_blk, m[i], 0.0) for i in items]
        t_inv = [eye + pw[i] for i in items]
        for _ in range(int(math.log2(DN_INV_BLOCK)) - 1):
            pwb = [pw[i].astype(BF16) for i in items]
            pw = [_dot(pwb[i], pwb[i]) for i in items]
            t_inv = [t_inv[i] + _dot(t_inv[i].astype(BF16), pw[i].astype(BF16)) for i in items]
        for off_blk in off_blks:
            tb = [t_inv[i].astype(BF16) for i in items]
            x = [_dot(jnp.where(off_blk, m[i], 0.0).astype(BF16), tb[i]).astype(BF16) for i in items]
            t_inv = [t_inv[i] + _dot(tb[i], x[i]) for i in items]
        rhs = [jnp.concatenate([v_ref[i % DN_HEADS, rows[i], :].astype(F32) * beta[i], kb[i] * egc[i]],
                               axis=1).astype(BF16) for i in items]
        sol = [_dot(t_inv[i].astype(BF16), rhs[i]) for i in items]
        qg = [(qq[i].astype(F32) * egc[i]).astype(BF16) for i in items]
        kd, eg_last = [], []
        for i in items:
            glast = gcol[i][c - 1:c, :]
            kd.append((kf[i] * jnp.exp(glast - gcol[i])).astype(BF16))
            eg_last.append(jnp.exp(glast))
        for u in range(DN_UNROLL):
            idx = [u * DN_HEADS + h for h in heads]
            st = [state[h] for h in heads]
            stb = [st[h].astype(BF16) for h in heads]
            v_new = [sol[i][:, :DN_HEAD_DIM] - _dot(sol[i][:, DN_HEAD_DIM:].astype(BF16), stb[h])
                     for h, i in enumerate(idx)]
            vnb = [v.astype(BF16) for v in v_new]
            for h, i in enumerate(idx):
                state[h] = st[h] * eg_last[i] + _dot_tn(kd[i], vnb[h])
            o = [_dot(qg[i], stb[h]) + _dot(qk[i], vnb[h]) for h, i in enumerate(idx)]
            for h, i in enumerate(idx):
                cols = slice(h * DN_HEAD_DIM, (h + 1) * DN_HEAD_DIM)
                on = o[h] * lax.rsqrt(jnp.mean(o[h] * o[h], axis=-1, keepdims=True) + NORM_EPS) * onorm
                y_ref[rows[i], cols] = (on * _silu(z_ref[rows[i], cols])).astype(BF16)
        return 0

    lax.fori_loop(0, ts // (c * DN_UNROLL), step, 0)


def _deltanet(q, k, v, z, ab, abt, prow, pcol, onorm, *, ts=1024):
    s = z.shape[0]
    head = pl.BlockSpec((DN_HEADS, ts, DN_HEAD_DIM), lambda i: (0, i, 0))
    nab = abt.shape[1]
    return pl.pallas_call(
        functools.partial(_dn_body, ts=ts),
        out_shape=jax.ShapeDtypeStruct((s, DN_WIDTH), BF16),
        grid=(s // ts,),
        in_specs=[head, head, head,
                  pl.BlockSpec((ts, DN_WIDTH), lambda i: (i, 0)),
                  pl.BlockSpec((ts, LANES), lambda i: (i, 0)),
                  pl.BlockSpec((ts // DN_CHUNK, nab, DN_CHUNK), lambda i: (i, 0, 0)),
                  _resident(prow.shape), _resident(pcol.shape), _resident(onorm.shape)],
        out_specs=pl.BlockSpec((ts, DN_WIDTH), lambda i: (i, 0)),
        scratch_shapes=[pltpu.VMEM((DN_HEADS, DN_HEAD_DIM, DN_HEAD_DIM), F32)],
        compiler_params=_cparams(("arbitrary",)),
        name="gated_deltanet",
    )(q, k, v, z, ab, abt, prow, pcol, onorm)


def _row(v):
    return v.reshape(1, -1).astype(F32)


def _block_diag(w):
    n, j, k = w.shape
    eye = jnp.eye(n, dtype=w.dtype)
    return (w[:, :, None, :] * eye[:, None, :, None]).reshape(n * j, n * k)


def kernel(x, mem, ffn1_norm, ffn1_w_in, ffn1_w_out, mix_norm, xa_norm, xa_mem_norm, xa_wq, xa_wkv, xa_wo,
           ffn2_norm, ffn2_w_in, ffn2_w_out, ab_w_in, lru_conv_w, lru_conv_b, lru_w_a, lru_b_a, lru_w_x,
           lru_b_x, lru_lambda, ab_w_out, dn_w_in, dn_conv_w, dn_a_log, dn_dt_bias, dn_o_norm, dn_w_out,
           final_norm):
    batch, seq, _ = x.shape
    depth = ffn1_norm.shape[0]
    outs = []
    for bi in range(batch):
        xs = x[bi]
        ms = mem[bi]
        for layer in range(depth):
            j = layer // 2
            last = layer == depth - 1
            xs = _ffn(xs, _row(ffn1_norm[layer]), ffn1_w_in, ffn1_w_out, _row(final_norm), layer,
                      final_norm=False)
            if layer % 2 == 0:
                q, k, v, xr, gr = _proj0(xs, _row(mix_norm[layer]), ab_w_in, j)
                attn = _attention(q, k, v)
                y = _lru(xr, gr, lru_conv_w[j].astype(F32), _row(lru_conv_b[j]),
                         _block_diag(lru_w_a[j]).astype(BF16), _row(lru_b_a[j]),
                         _block_diag(lru_w_x[j]).astype(BF16), _row(lru_b_x[j]), _row(lru_lambda[j]))
                mix_ys, mix_w = (attn, y), ab_w_out
            else:
                w = dn_w_in[j]
                wab = w[:, 4 * DN_WIDTH:]
                n_ab = wab.shape[1]
                wab_pad = jnp.pad(wab, ((0, 0), (0, LANES - n_ab))).astype(BF16)
                q, k, v, z, ab, abt = _dnprep(
                    xs, _row(mix_norm[layer]), dn_w_in, j, wab_pad, wab.T.astype(BF16),
                    dn_conv_w[j].astype(F32))
                pad = jnp.zeros((LANES - DN_HEADS,), F32)
                prow = jnp.stack([jnp.concatenate([dn_a_log[j].astype(F32), pad]),
                                  jnp.concatenate([dn_dt_bias[j].astype(F32), pad])])
                pad2 = jnp.zeros((n_ab - DN_HEADS,), F32)
                pcol = jnp.stack([jnp.concatenate([dn_a_log[j].astype(F32), pad2]),
                                  jnp.concatenate([dn_dt_bias[j].astype(F32), pad2])], axis=1)
                y = _deltanet(q, k, v, z, ab, abt, prow, pcol, _row(dn_o_norm[j]))
                mix_ys, mix_w = (y,), dn_w_out
            xs = _xattn(xs, mix_ys, mix_w, j, _row(xa_norm[layer]), xa_wq, ms, _row(xa_mem_norm[layer]),
                        xa_wkv, xa_wo, layer)
            xs = _ffn(xs, _row(ffn2_norm[layer]), ffn2_w_in, ffn2_w_out, _row(final_norm), layer,
                      final_norm=last)
        outs.append(xs)
    return outs[0][None] if batch == 1 else jnp.stack(outs, axis=0)
```
